```python
import math
import jax, jax.numpy as jnp
from jax import lax
import numpy as np

D_MODEL = 1024
BATCH = 8
SEQ = 4096
DEPTH = 2

N_HEADS_A = 8
HEAD_DIM_A = 128
CONV_A = 4
CHUNK = 64
N_HEADS_B = 16
N_KV_B = 4
HEAD_DIM_B = 64
WINDOW = 128
D_FF = 2816
FFN_CONV = 3
PLE_DIM = 256
EPS = 1e-6
N_A_LAYERS = (DEPTH + 1) // 2
N_B_LAYERS = DEPTH // 2

kernel_name = "hybrid_deltanet_swa_convffn_ple"


def rmsnorm(x, w):
    x32 = x.astype(jnp.float32)
    y = x32 * lax.rsqrt(jnp.mean(x32 * x32, axis=-1, keepdims=True) + EPS)
    return (y * w.astype(jnp.float32)).astype(x.dtype)


def l2norm(x):
    x32 = x.astype(jnp.float32)
    return x32 * lax.rsqrt(jnp.sum(x32 * x32, axis=-1, keepdims=True) + EPS)


def causal_dwconv(x, w):
    k_w = w.shape[0]
    t = x.shape[1]
    xp = jnp.pad(x, ((0, 0), (k_w - 1, 0), (0, 0)))
    y = w[0] * xp[:, 0:t]
    for i in range(1, k_w):
        y = y + w[i] * xp[:, i:i + t]
    return y


def gated_delta_rule(q, k, v, g, beta):
    b_, t_, h_, dk = q.shape
    dv = v.shape[-1]
    n_ch = t_ // CHUNK
    f32 = jnp.float32

    def chunks(a):
        return a.astype(f32).reshape(b_, n_ch, CHUNK, h_, a.shape[-1]).transpose(1, 0, 3, 2, 4)

    qc = chunks(q) * (dk ** -0.5)
    kc = chunks(k)
    vc = chunks(v)
    gcs = jnp.cumsum(g.astype(f32).reshape(b_, n_ch, CHUNK, h_).transpose(1, 0, 3, 2), axis=-1)
    bc = beta.astype(f32).reshape(b_, n_ch, CHUNK, h_).transpose(1, 0, 3, 2)

    idx = jnp.arange(CHUNK)
    incl = idx[:, None] >= idx[None, :]
    strict = idx[:, None] > idx[None, :]
    diff = gcs[..., :, None] - gcs[..., None, :]
    decay = jnp.exp(jnp.where(incl, diff, -jnp.inf))

    kk = jnp.einsum('nbhik,nbhjk->nbhij', kc, kc)
    lmat = jnp.where(strict, bc[..., :, None] * kk * decay, 0.0)
    eye_l = jnp.eye(CHUNK, dtype=f32) + lmat
    rhs = jnp.concatenate([bc[..., None] * vc,
                           (bc * jnp.exp(gcs))[..., None] * kc], axis=-1)
    sol = lax.linalg.triangular_solve(eye_l, rhs, left_side=True, lower=True, unit_diagonal=True)
    u0 = sol[..., :dv]
    wk = sol[..., dv:]

    qk = jnp.einsum('nbhik,nbhjk->nbhij', qc, kc) * decay
    q_dec = qc * jnp.exp(gcs)[..., None]
    k_dec = kc * jnp.exp(gcs[..., -1:] - gcs)[..., None]
    g_last = jnp.exp(gcs[..., -1])

    def step(s, inp):
        qd, kd, qkc, u0c, wc, gl = inp
        u = u0c - jnp.einsum('bhck,bhkv->bhcv', wc, s)
        o = jnp.einsum('bhck,bhkv->bhcv', qd, s) + jnp.einsum('bhij,bhjv->bhiv', qkc, u)
        s = gl[..., None, None] * s + jnp.einsum('bhck,bhcv->bhkv', kd, u)
        return s, o

    s0 = jnp.zeros((b_, h_, dk, dv), f32)
    _, o = lax.scan(step, s0, (q_dec, k_dec, qk, u0, wk, g_last))
    return o.transpose(1, 0, 3, 2, 4).reshape(b_, t_, h_, dv)


def deltanet_mixer(x, w_in, conv_w, a_log, dt_bias, norm_w, w_out):
    b_, t_, _ = x.shape
    hk = N_HEADS_A * HEAD_DIM_A
    proj = x @ w_in
    qkv = jax.nn.silu(causal_dwconv(proj[..., :3 * hk], conv_w))
    z = proj[..., 3 * hk:4 * hk].reshape(b_, t_, N_HEADS_A, HEAD_DIM_A)
    b_logit = proj[..., 4 * hk:4 * hk + N_HEADS_A]
    a_in = proj[..., 4 * hk + N_HEADS_A:]
    q = l2norm(qkv[..., :hk].reshape(b_, t_, N_HEADS_A, HEAD_DIM_A))
    k = l2norm(qkv[..., hk:2 * hk].reshape(b_, t_, N_HEADS_A, HEAD_DIM_A))
    v = qkv[..., 2 * hk:].reshape(b_, t_, N_HEADS_A, HEAD_DIM_A)
    beta = jax.nn.sigmoid(b_logit.astype(jnp.float32))
    g = -jnp.exp(a_log.astype(jnp.float32)) * jax.nn.softplus(a_in.astype(jnp.float32) + dt_bias.astype(jnp.float32))
    o = gated_delta_rule(q, k, v, g, beta).astype(x.dtype)
    o = rmsnorm(o, norm_w) * jax.nn.silu(z)
    return o.reshape(b_, t_, hk) @ w_out


def alibi_slopes(n_heads):
    return 2.0 ** (-8.0 * jnp.arange(1, n_heads + 1, dtype=jnp.float32) / n_heads)


def swa_mixer(x, w_in, sinks, w_out):
    b_, t_, _ = x.shape
    n_blk = t_ // WINDOW
    grp = N_HEADS_B // N_KV_B
    qd = N_HEADS_B * HEAD_DIM_B
    kd = N_KV_B * HEAD_DIM_B
    proj = x @ w_in
    q = proj[..., :qd].reshape(b_, n_blk, WINDOW, N_KV_B, grp, HEAD_DIM_B)
    k = proj[..., qd:qd + kd].reshape(b_, n_blk, WINDOW, N_KV_B, HEAD_DIM_B)
    v = proj[..., qd + kd:].reshape(b_, n_blk, WINDOW, N_KV_B, HEAD_DIM_B)

    def with_prev(a):
        prev = jnp.pad(a[:, :-1], ((0, 0), (1, 0), (0, 0), (0, 0), (0, 0)))
        return jnp.concatenate([prev, a], axis=2)

    kb, vb = with_prev(k), with_prev(v)
    scores = jnp.einsum('bnqhgd,bnkhd->bhgnqk', q, kb).astype(jnp.float32) * (HEAD_DIM_B ** -0.5)
    qi = jnp.arange(WINDOW)[:, None]
    kj = jnp.arange(2 * WINDOW)[None, :]
    dist = qi + WINDOW - kj
    blk = jnp.arange(n_blk)[:, None, None]
    valid = (dist >= 0) & (dist < WINDOW) & (blk * WINDOW - WINDOW + kj >= 0)
    slopes = alibi_slopes(N_HEADS_B).reshape(N_KV_B, grp)
    logits = scores - slopes[None, :, :, None, None, None] * dist.astype(jnp.float32)
    logits = jnp.where(valid, logits, -jnp.inf)
    sink = sinks.astype(jnp.float32).reshape(1, N_KV_B, grp, 1, 1, 1)
    m = jnp.maximum(jnp.max(logits, axis=-1, keepdims=True), sink)
    e = jnp.exp(logits - m)
    probs = e / (jnp.sum(e, axis=-1, keepdims=True) + jnp.exp(sink - m))
    out = jnp.einsum('bhgnqk,bnkhd->bnqhgd', probs.astype(x.dtype), vb).reshape(b_, t_, qd)
    return out @ w_out


def conv_ffn(x, w_up, conv_w, w_down):
    u = causal_dwconv(x @ w_up, conv_w)
    gate, val = u[..., :D_FF], u[..., D_FF:]
    return (jax.nn.silu(gate) * val) @ w_down


def setup_inputs(seed: int = 0) -> dict:
    key = jax.random.key(seed)
    ks = jax.random.split(key, 24)
    f32 = jnp.float32
    hk = N_HEADS_A * HEAD_DIM_A

    def dense(k, shape, fan_in):
        return jax.random.normal(k, shape, f32) * (fan_in ** -0.5)

    def gain(k, shape):
        return 1.0 + 0.02 * jax.random.normal(k, shape, f32)

    x = jax.random.normal(ks[0], (BATCH, SEQ, D_MODEL), f32)
    p = jax.random.normal(ks[1], (DEPTH, BATCH, SEQ, PLE_DIM), f32)
    norm_mix = gain(ks[2], (DEPTH, D_MODEL))
    norm_ffn = gain(ks[3], (DEPTH, D_MODEL))
    norm_ple = gain(ks[4], (DEPTH, D_MODEL))
    norm_final = gain(ks[5], (D_MODEL,))

    a_w_in = dense(ks[6], (N_A_LAYERS, D_MODEL, 4 * hk + 2 * N_HEADS_A), D_MODEL)
    a_conv = jax.random.normal(ks[7], (N_A_LAYERS, CONV_A, 3 * hk), f32) * (CONV_A ** -0.5)
    a_log = jnp.log(jax.random.uniform(ks[8], (N_A_LAYERS, N_HEADS_A), f32, 1.0, 16.0))
    dt = jnp.exp(jax.random.uniform(ks[9], (N_A_LAYERS, N_HEADS_A), f32, math.log(1e-3), math.log(1e-1)))
    a_dt_bias = dt + jnp.log(-jnp.expm1(-dt))
    a_norm = gain(ks[10], (N_A_LAYERS, HEAD_DIM_A))
    a_w_out = dense(ks[11], (N_A_LAYERS, hk, D_MODEL), hk)

    qd = N_HEADS_B * HEAD_DIM_B
    kd = N_KV_B * HEAD_DIM_B
    b_w_in = dense(ks[12], (N_B_LAYERS, D_MODEL, qd + 2 * kd), D_MODEL)
    b_sinks = jax.random.normal(ks[13], (N_B_LAYERS, N_HEADS_B), f32)
    b_w_out = dense(ks[14], (N_B_LAYERS, qd, D_MODEL), qd)

    f_w_up = dense(ks[15], (DEPTH, D_MODEL, 2 * D_FF), D_MODEL)
    f_conv = jax.random.normal(ks[16], (DEPTH, FFN_CONV, 2 * D_FF), f32) * (FFN_CONV ** -0.5)
    f_w_down = dense(ks[17], (DEPTH, D_FF, D_MODEL), D_FF)

    ple_w_proj = dense(ks[18], (DEPTH, PLE_DIM, D_MODEL), PLE_DIM)
    ple_w_gate = dense(ks[19], (DEPTH, D_MODEL, D_MODEL), D_MODEL)

    return {"x": x, "p": p, "norm_mix": norm_mix, "norm_ffn": norm_ffn,
            "norm_ple": norm_ple, "norm_final": norm_final,
            "a_w_in": a_w_in, "a_conv": a_conv, "a_log": a_log, "a_dt_bias": a_dt_bias,
            "a_norm": a_norm, "a_w_out": a_w_out,
            "b_w_in": b_w_in, "b_sinks": b_sinks, "b_w_out": b_w_out,
            "f_w_up": f_w_up, "f_conv": f_conv, "f_w_down": f_w_down,
            "ple_w_proj": ple_w_proj, "ple_w_gate": ple_w_gate}


def reference(x, p, norm_mix, norm_ffn, norm_ple, norm_final,
              a_w_in, a_conv, a_log, a_dt_bias, a_norm, a_w_out,
              b_w_in, b_sinks, b_w_out,
              f_w_up, f_conv, f_w_down,
              ple_w_proj, ple_w_gate):
    h = x
    for i in range(DEPTH):
        hn = rmsnorm(h, norm_mix[i])
        j = i // 2
        if i % 2 == 0:
            mix = deltanet_mixer(hn, a_w_in[j], a_conv[j], a_log[j], a_dt_bias[j], a_norm[j], a_w_out[j])
        else:
            mix = swa_mixer(hn, b_w_in[j], b_sinks[j], b_w_out[j])
        h = h + mix
        h = h + conv_ffn(rmsnorm(h, norm_ffn[i]), f_w_up[i], f_conv[i], f_w_down[i])
        gate = jax.nn.sigmoid(rmsnorm(h, norm_ple[i]) @ ple_w_gate[i])
        h = h + gate * (p[i] @ ple_w_proj[i])
    return rmsnorm(h, norm_final)
```

```python
import functools
import math

import jax
import jax.numpy as jnp
from jax import lax
from jax.experimental import pallas as pl
from jax.experimental.pallas import tpu as pltpu

F32 = jnp.float32
BF16 = jnp.bfloat16

D_MODEL = 1024
N_HEADS_A = 8
HEAD_DIM_A = 128
CONV_A = 4
N_HEADS_B = 16
N_KV_B = 4
HEAD_DIM_B = 64
WINDOW = 128
D_FF = 2816
FFN_CONV = 3
PLE_DIM = 256
EPS = 1e-6

LANES = 128
BF16_ROWS = 16
VMEM_LIMIT = 56 * 1024 * 1024

DELTA_CHUNK = 128
DELTA_TILE = 1024
DELTA_HEADS = 2


def _cparams(sem):
    return pltpu.CompilerParams(dimension_semantics=sem, vmem_limit_bytes=VMEM_LIMIT)


def _rms(x, gain):
    ms = jnp.mean(x * x, axis=-1, keepdims=True)
    return x * lax.rsqrt(ms + EPS) * gain


def _sigmoid(x):
    return 1.0 / (1.0 + jnp.exp(-x))


def _silu(x):
    return x * _sigmoid(x)


def _rms_matmul_kernel(x_ref, g_ref, w_ref, o_ref, *, n_chunk):
    xn = _rms(x_ref[...], g_ref[...]).astype(BF16)
    n = o_ref.shape[1]
    for c in range(0, n, n_chunk):
        w = min(n_chunk, n - c)
        o_ref[:, c:c + w] = jnp.dot(xn, w_ref[:, c:c + w], preferred_element_type=F32).astype(o_ref.dtype)


def rms_matmul(x, gain, w, *, tm, out_dtype=F32):
    m, d = x.shape
    n = w.shape[1]
    return pl.pallas_call(
        functools.partial(_rms_matmul_kernel, n_chunk=512),
        grid=(m // tm,),
        in_specs=[pl.BlockSpec((tm, d), lambda i: (i, 0)),
                  pl.BlockSpec((1, d), lambda i: (0, 0)),
                  pl.BlockSpec((d, n), lambda i: (0, 0))],
        out_specs=pl.BlockSpec((tm, n), lambda i: (i, 0)),
        out_shape=jax.ShapeDtypeStruct((m, n), out_dtype),
        compiler_params=_cparams(("parallel",)),
        name="rms_matmul",
    )(x, gain.reshape(1, d), w)


def _matmul_residual_kernel(a_ref, w_ref, h_ref, o_ref):
    o_ref[...] = h_ref[...] + jnp.dot(a_ref[...], w_ref[...], preferred_element_type=F32)


def matmul_residual(a, w, h, *, tm):
    m, k = a.shape
    n = w.shape[1]
    return pl.pallas_call(
        _matmul_residual_kernel,
        grid=(m // tm,),
        in_specs=[pl.BlockSpec((tm, k), lambda i: (i, 0)),
                  pl.BlockSpec((k, n), lambda i: (0, 0)),
                  pl.BlockSpec((tm, n), lambda i: (i, 0))],
        out_specs=pl.BlockSpec((tm, n), lambda i: (i, 0)),
        out_shape=jax.ShapeDtypeStruct((m, n), F32),
        compiler_params=_cparams(("parallel",)),
        name="matmul_residual",
    )(a, w, h)


def _conv_ffn_kernel(h_ref, halo_ref, g_ref, wup_ref, cw_ref, wd_ref, o_ref,
                     xn_ref, ug_ref, uv_ref, acc_ref, *, tiles_per_seq, fc):
    tm = h_ref.shape[0]
    hb = halo_ref.shape[0]
    gain = g_ref[...]
    first = (pl.program_id(0) % tiles_per_seq) == 0
    halo = _rms(halo_ref[...], gain)
    xn_ref[0:hb, :] = jnp.where(first, 0.0, halo).astype(BF16)
    xn_ref[hb:, :] = _rms(h_ref[...], gain).astype(BF16)
    xn = xn_ref[...]

    def conv(u_ref, col):
        w = cw_ref[:, col:col + fc]
        y = w[0:1, :] * u_ref[hb - 2:hb - 2 + tm, :]
        y = y + w[1:2, :] * u_ref[hb - 1:hb - 1 + tm, :]
        return y + w[2:3, :] * u_ref[hb:hb + tm, :]

    for c in range(D_FF // fc):
        cg, cv = c * fc, D_FF + c * fc
        ug_ref[...] = jnp.dot(xn, wup_ref[:, cg:cg + fc], preferred_element_type=F32)
        uv_ref[...] = jnp.dot(xn, wup_ref[:, cv:cv + fc], preferred_element_type=F32)
        act = (_silu(conv(ug_ref, cg)) * conv(uv_ref, cv)).astype(BF16)
        part = jnp.dot(act, wd_ref[cg:cg + fc, :], preferred_element_type=F32)
        if c == 0:
            acc_ref[...] = part
        else:
            acc_ref[...] += part
    o_ref[...] = h_ref[...] + acc_ref[...]


def conv_ffn(h, gain, w_up, conv_w, w_down, *, tm, seq):
    m, d = h.shape
    hb = BF16_ROWS
    fc = 256
    tiles_per_seq = seq // tm
    return pl.pallas_call(
        functools.partial(_conv_ffn_kernel, tiles_per_seq=tiles_per_seq, fc=fc),
        grid=(m // tm,),
        in_specs=[pl.BlockSpec((tm, d), lambda i: (i, 0)),
                  pl.BlockSpec((hb, d), lambda i: (jnp.maximum(i * (tm // hb) - 1, 0), 0)),
                  pl.BlockSpec((1, d), lambda i: (0, 0)),
                  pl.BlockSpec((d, 2 * D_FF), lambda i: (0, 0), pipeline_mode=pl.Buffered(1)),
                  pl.BlockSpec((FFN_CONV, 2 * D_FF), lambda i: (0, 0)),
                  pl.BlockSpec((D_FF, d), lambda i: (0, 0), pipeline_mode=pl.Buffered(1))],
        out_specs=pl.BlockSpec((tm, d), lambda i: (i, 0)),
        out_shape=jax.ShapeDtypeStruct((m, d), F32),
        scratch_shapes=[pltpu.VMEM((tm + hb, d), BF16),
                        pltpu.VMEM((tm + hb, fc), F32),
                        pltpu.VMEM((tm + hb, fc), F32),
                        pltpu.VMEM((tm, d), F32)],
        compiler_params=_cparams(("parallel",)),
        name="conv_ffn",
    )(h, h, gain.reshape(1, d), w_up, conv_w, w_down)


def _ple_kernel(h_ref, p_ref, g_ref, wg_ref, wp_ref, gf_ref, o_ref, *, final):
    h = h_ref[...]
    xn = _rms(h, g_ref[...]).astype(BF16)
    gate = _sigmoid(jnp.dot(xn, wg_ref[...], preferred_element_type=F32))
    emb = jnp.dot(p_ref[...].astype(BF16), wp_ref[...], preferred_element_type=F32)
    out = h + gate * emb
    if final:
        out = _rms(out, gf_ref[...])
    o_ref[...] = out


def ple(h, p, gain, w_gate, w_proj, gain_final, *, tm, final):
    m, d = h.shape
    pd = p.shape[1]
    return pl.pallas_call(
        functools.partial(_ple_kernel, final=final),
        grid=(m // tm,),
        in_specs=[pl.BlockSpec((tm, d), lambda i: (i, 0)),
                  pl.BlockSpec((tm, pd), lambda i: (i, 0)),
                  pl.BlockSpec((1, d), lambda i: (0, 0)),
                  pl.BlockSpec((d, d), lambda i: (0, 0)),
                  pl.BlockSpec((pd, d), lambda i: (0, 0)),
                  pl.BlockSpec((1, d), lambda i: (0, 0))],
        out_specs=pl.BlockSpec((tm, d), lambda i: (i, 0)),
        out_shape=jax.ShapeDtypeStruct((m, d), F32),
        compiler_params=_cparams(("parallel",)),
        name="ple",
    )(h, p, gain.reshape(1, d), w_gate, w_proj, gain_final.reshape(1, d))


def _lower_unit_inverse(lmat, ii, jj):
    c = lmat.shape[0]
    eye = (ii == jj).astype(F32)
    t = None
    b = 1
    while b < c:
        sel = ((ii // (2 * b)) == (jj // (2 * b))) & (((ii // b) % 2) == 1) & (((jj // b) % 2) == 0)
        x = jnp.where(sel, lmat, 0.0)
        if t is None:
            t = eye - x
        else:
            xt = jnp.dot(x.astype(BF16), t.astype(BF16), preferred_element_type=F32)
            t = t - jnp.dot(t.astype(BF16), xt.astype(BF16), preferred_element_type=F32)
        b *= 2
    return t


def _delta_kernel(q_ref, k_ref, v_ref, z_ref, ba_ref, cw_ref, al_ref, dtb_ref, nw_ref, o_ref,
                  ext_ref, qn_ref, kn_ref, vn_ref, gcol_ref, grow_ref, s_ref, *, heads, chunk):
    tt = q_ref.shape[0]
    hw = heads * HEAD_DIM_A
    hg = pl.program_id(1)
    tstep = pl.program_id(2)
    pad = 8

    @pl.when(tstep == 0)
    def _():
        ext_ref[0:pad, :] = jnp.zeros((pad, 3 * hw), F32)
        s_ref[...] = jnp.zeros(s_ref.shape, F32)

    ext_ref[pad:, 0:hw] = q_ref[...]
    ext_ref[pad:, hw:2 * hw] = k_ref[...]
    ext_ref[pad:, 2 * hw:3 * hw] = v_ref[...]
    cw = cw_ref[0]
    for part, dst in ((0, qn_ref), (1, kn_ref), (2, vn_ref)):
        lo = part * hw
        y = cw[part, 0:1, :] * ext_ref[pad - 3:pad - 3 + tt, lo:lo + hw]
        for i in range(1, CONV_A):
            y = y + cw[part, i:i + 1, :] * ext_ref[pad - 3 + i:pad - 3 + i + tt, lo:lo + hw]
        y = _silu(y)
        if part < 2:
            for hh in range(heads):
                yh = y[:, hh * HEAD_DIM_A:(hh + 1) * HEAD_DIM_A]
                yh = yh * lax.rsqrt(jnp.sum(yh * yh, axis=-1, keepdims=True) + EPS)
                if part == 0:
                    yh = yh * (HEAD_DIM_A ** -0.5)
                dst[:, hh * HEAD_DIM_A:(hh + 1) * HEAD_DIM_A] = yh
        else:
            dst[...] = y
    ext_ref[0:pad, :] = ext_ref[tt:tt + pad, :]

    ba = ba_ref[...]
    lane = lax.broadcasted_iota(jnp.int32, ba.shape, 1)
    sp = ba + dtb_ref[...]
    softplus = jnp.maximum(sp, 0.0) + jnp.log1p(jnp.exp(-jnp.abs(sp)))
    gb = jnp.where(lane < N_HEADS_A, _sigmoid(ba), -jnp.exp(al_ref[...]) * softplus)
    gcol_ref[:, 0:LANES] = gb
    gt = gb.T
    tl = lax.broadcasted_iota(jnp.int32, gt.shape, 1) % chunk
    sh = 1
    while sh < chunk:
        gt = gt + jnp.where(tl >= sh, pltpu.roll(gt, sh, 1), 0.0)
        sh *= 2
    grow_ref[...] = gt
    gcol_ref[:, LANES:2 * LANES] = gt.T

    ii = lax.broadcasted_iota(jnp.int32, (chunk, chunk), 0)
    jj = lax.broadcasted_iota(jnp.int32, (chunk, chunk), 1)
    lane_c = lax.broadcasted_iota(jnp.int32, (chunk, LANES), 1)
    row_h = lax.broadcasted_iota(jnp.int32, (N_HEADS_A, chunk), 0)
    nw = nw_ref[...]

    def chunk_body(c, carry):
        r0 = pl.multiple_of(c * chunk, chunk)
        gbc = gcol_ref[pl.ds(r0, chunk), 0:LANES]
        gcc = gcol_ref[pl.ds(r0, chunk), LANES:2 * LANES]
        grc = grow_ref[N_HEADS_A:2 * N_HEADS_A, pl.ds(r0, chunk)]
        for hh in range(heads):
            h = hg * heads + hh
            cs = slice(hh * HEAD_DIM_A, (hh + 1) * HEAD_DIM_A)
            q = qn_ref[pl.ds(r0, chunk), cs]
            k = kn_ref[pl.ds(r0, chunk), cs]
            v = vn_ref[pl.ds(r0, chunk), cs]
            beta = jnp.sum(jnp.where(lane_c == h, gbc, 0.0), axis=1, keepdims=True)
            gc = jnp.sum(jnp.where(lane_c == N_HEADS_A + h, gcc, 0.0), axis=1, keepdims=True)
            gr = jnp.sum(jnp.where(row_h == h, grc, 0.0), axis=0, keepdims=True)
            g_last = gr[:, chunk - 1:chunk]
            decay = jnp.where(ii >= jj, jnp.exp(gc - gr), 0.0)
            eg = jnp.exp(gc)

            kb = k.astype(BF16)
            kq = lax.dot_general(jnp.concatenate([kb, q.astype(BF16)], axis=0), kb,
                                 (((1,), (1,)), ((), ())), preferred_element_type=F32)
            kk, qk = kq[:chunk], kq[chunk:]
            lmat = jnp.where(ii > jj, beta * kk * decay, 0.0)
            attn = qk * decay
            t = _lower_unit_inverse(lmat, ii, jj)
            rhs = jnp.concatenate([beta * v, (beta * eg) * k], axis=1).astype(BF16)
            sol = jnp.dot(t.astype(BF16), rhs, preferred_element_type=F32)
            u0, wk = sol[:, :HEAD_DIM_A], sol[:, HEAD_DIM_A:]

            s = s_ref[hh]
            sb = s.astype(BF16)
            ws = jnp.dot(jnp.concatenate([wk, q * eg], axis=0).astype(BF16), sb,
                         preferred_element_type=F32)
            u = u0 - ws[:chunk]
            ub = u.astype(BF16)
            o = ws[chunk:] + jnp.dot(attn.astype(BF16), ub, preferred_element_type=F32)
            k_dec = (k * jnp.exp(g_last - gc)).astype(BF16)
            s_ref[hh] = jnp.exp(g_last) * s + lax.dot_general(
                k_dec, ub, (((0,), (0,)), ((), ())), preferred_element_type=F32)

            z = z_ref[pl.ds(r0, chunk), cs]
            o_ref[pl.ds(r0, chunk), cs] = (_rms(o, nw) * _silu(z)).astype(o_ref.dtype)
        return carry

    lax.fori_loop(0, tt // chunk, chunk_body, 0)


def delta_mixer(proj, conv_w, a_log, dt_bias, norm_w, *, batch, seq):
    heads, chunk = DELTA_HEADS, DELTA_CHUNK
    tt = min(DELTA_TILE, seq)
    hk = N_HEADS_A * HEAD_DIM_A
    hw = heads * HEAD_DIM_A
    n_hg = N_HEADS_A // heads
    n_t = seq // tt
    m = batch * seq
    cw = conv_w.reshape(CONV_A, 3, n_hg, hw).transpose(2, 1, 0, 3)
    pad_row = jnp.zeros((1, LANES), F32)
    al = pad_row.at[0, N_HEADS_A:2 * N_HEADS_A].set(a_log.astype(F32))
    dtb = pad_row.at[0, N_HEADS_A:2 * N_HEADS_A].set(dt_bias.astype(F32))

    def col(off):
        return lambda b, g, t: (b * n_t + t, off * n_hg + g)

    return pl.pallas_call(
        functools.partial(_delta_kernel, heads=heads, chunk=chunk),
        grid=(batch, n_hg, n_t),
        in_specs=[pl.BlockSpec((tt, hw), col(0)),
                  pl.BlockSpec((tt, hw), col(1)),
                  pl.BlockSpec((tt, hw), col(2)),
                  pl.BlockSpec((tt, hw), col(3)),
                  pl.BlockSpec((tt, LANES), lambda b, g, t: (b * n_t + t, 4 * hk // LANES)),
                  pl.BlockSpec((1, 3, CONV_A, hw), lambda b, g, t: (g, 0, 0, 0)),
                  pl.BlockSpec((1, LANES), lambda b, g, t: (0, 0)),
                  pl.BlockSpec((1, LANES), lambda b, g, t: (0, 0)),
                  pl.BlockSpec((1, HEAD_DIM_A), lambda b, g, t: (0, 0))],
        out_specs=pl.BlockSpec((tt, hw), lambda b, g, t: (b * n_t + t, g)),
        out_shape=jax.ShapeDtypeStruct((m, hk), BF16),
        scratch_shapes=[pltpu.VMEM((tt + 8, 3 * hw), F32),
                        pltpu.VMEM((tt, hw), F32),
                        pltpu.VMEM((tt, hw), F32),
                        pltpu.VMEM((tt, hw), F32),
                        pltpu.VMEM((tt, 2 * LANES), F32),
                        pltpu.VMEM((LANES, tt), F32),
                        pltpu.VMEM((heads, HEAD_DIM_A, HEAD_DIM_A), F32)],
        compiler_params=_cparams(("parallel", "parallel", "arbitrary")),
        name="delta_mixer",
    )(proj, proj, proj, proj, proj, cw, al, dtb, norm_w.reshape(1, HEAD_DIM_A))


def _swa_kernel(sink_ref, q_ref, kc_ref, vc_ref, kp_ref, vp_ref, o_ref, *, n_blk):
    w = WINDOW
    grp = N_HEADS_B // N_KV_B
    hd = HEAD_DIM_B
    blk = pl.program_id(0) % n_blk
    qi = lax.broadcasted_iota(jnp.int32, (w, 2 * w), 0)
    kj = lax.broadcasted_iota(jnp.int32, (w, 2 * w), 1)
    dist = qi + w - kj
    valid = (dist >= 0) & (dist < w) & ((kj >= w) | (blk > 0))
    distf = dist.astype(F32)
    scale = hd ** -0.5
    for h in range(N_KV_B):
        kb = jnp.concatenate([kp_ref[:, h * hd:(h + 1) * hd], kc_ref[:, h * hd:(h + 1) * hd]], axis=0)
        vb = jnp.concatenate([vp_ref[:, h * hd:(h + 1) * hd], vc_ref[:, h * hd:(h + 1) * hd]], axis=0)
        qs = jnp.concatenate([q_ref[:, (h * grp + g) * hd:(h * grp + g + 1) * hd] for g in range(grp)], axis=0)
        s = lax.dot_general(qs.astype(BF16), kb.astype(BF16), (((1,), (1,)), ((), ())),
                            preferred_element_type=F32) * scale
        es = []
        rs = []
        for g in range(grp):
            head = h * grp + g
            slope = 2.0 ** (-8.0 * (head + 1) / N_HEADS_B)
            sink = sink_ref[head]
            logits = jnp.where(valid, s[g * w:(g + 1) * w] - slope * distf, -jnp.inf)
            mx = jnp.maximum(jnp.max(logits, axis=-1, keepdims=True), sink)
            e = jnp.exp(logits - mx)
            den = jnp.sum(e, axis=-1, keepdims=True) + jnp.exp(sink - mx)
            es.append(e.astype(BF16))
            rs.append(1.0 / den)
        pv = jnp.dot(jnp.concatenate(es, axis=0), vb.astype(BF16), preferred_element_type=F32)
        for g in range(grp):
            head = h * grp + g
            o_ref[:, head * hd:(head + 1) * hd] = (pv[g * w:(g + 1) * w] * rs[g]).astype(o_ref.dtype)


def swa_mixer(proj, sinks, *, batch, seq):
    w = WINDOW
    n_blk = seq // w
    qd = N_HEADS_B * HEAD_DIM_B
    kd = N_KV_B * HEAD_DIM_B
    m = batch * seq
    kcol, vcol = qd // kd, qd // kd + 1
    prev = lambda i: jnp.maximum(i - 1, 0)
    return pl.pallas_call(
        functools.partial(_swa_kernel, n_blk=n_blk),
        grid=(m // w,),
        in_specs=[pl.BlockSpec(memory_space=pltpu.SMEM),
                  pl.BlockSpec((w, qd), lambda i: (i, 0)),
                  pl.BlockSpec((w, kd), lambda i: (i, kcol)),
                  pl.BlockSpec((w, kd), lambda i: (i, vcol)),
                  pl.BlockSpec((w, kd), lambda i: (prev(i), kcol)),
                  pl.BlockSpec((w, kd), lambda i: (prev(i), vcol))],
        out_specs=pl.BlockSpec((w, qd), lambda i: (i, 0)),
        out_shape=jax.ShapeDtypeStruct((m, qd), BF16),
        compiler_params=_cparams(("parallel",)),
        name="swa_mixer",
    )(sinks.astype(F32), proj, proj, proj, proj, proj)


def kernel(x, p, norm_mix, norm_ffn, norm_ple, norm_final, a_w_in, a_conv, a_log, a_dt_bias, a_norm, a_w_out,
           b_w_in, b_sinks, b_w_out, f_w_up, f_conv, f_w_down, ple_w_proj, ple_w_gate):
    batch, seq, d = x.shape
    depth = p.shape[0]
    m = batch * seq
    hk = N_HEADS_A * HEAD_DIM_A
    h = x.reshape(m, d)
    for i in range(depth):
        j = i // 2
        if i % 2 == 0:
            w_in = a_w_in[j]
            w_in = jnp.pad(w_in, ((0, 0), (0, 4 * hk + LANES - w_in.shape[1]))).astype(BF16)
            proj = rms_matmul(h, norm_mix[i], w_in, tm=256)
            mix = delta_mixer(proj, a_conv[j], a_log[j], a_dt_bias[j], a_norm[j], batch=batch, seq=seq)
            w_out = a_w_out[j]
        else:
            proj = rms_matmul(h, norm_mix[i], b_w_in[j].astype(BF16), tm=512)
            mix = swa_mixer(proj, b_sinks[j], batch=batch, seq=seq)
            w_out = b_w_out[j]
        h = matmul_residual(mix, w_out.astype(BF16), h, tm=512)
        h = conv_ffn(h, norm_ffn[i], f_w_up[i].astype(BF16), f_conv[i], f_w_down[i].astype(BF16),
                     tm=512, seq=seq)
        h = ple(h, p[i].reshape(m, -1), norm_ple[i], ple_w_gate[i].astype(BF16), ple_w_proj[i].astype(BF16),
                norm_final, tm=512, final=(i == depth - 1))
    return h.reshape(batch, seq, d)
```

```python
import functools
import math

import jax
import jax.numpy as jnp
from jax import lax
from jax.experimental import pallas as pl
from jax.experimental.pallas import tpu as pltpu

F32 = jnp.float32
BF16 = jnp.bfloat16

D_MODEL = 1024
N_HEADS_A = 8
HEAD_DIM_A = 128
CONV_A = 4
N_HEADS_B = 16
N_KV_B = 4
HEAD_DIM_B = 64
WINDOW = 128
D_FF = 2816
FFN_CONV = 3
PLE_DIM = 256
EPS = 1e-6

LANES = 128
BF16_ROWS = 16
VMEM_LIMIT = 56 * 1024 * 1024

DELTA_CHUNK = 128
DELTA_TILE = 1024
DELTA_HEADS = 2


def _cparams(sem):
    return pltpu.CompilerParams(dimension_semantics=sem, vmem_limit_bytes=VMEM_LIMIT)


def _rms(x, gain):
    ms = jnp.mean(x * x, axis=-1, keepdims=True)
    return x * lax.rsqrt(ms + EPS) * gain


def _sigmoid(x):
    return 1.0 / (1.0 + jnp.exp(-x))


def _silu(x):
    return x * _sigmoid(x)


def _rms_matmul_kernel(x_ref, g_ref, w_ref, o_ref, *, n_chunk):
    xn = _rms(x_ref[...], g_ref[...]).astype(BF16)
    n = o_ref.shape[1]
    for c in range(0, n, n_chunk):
        w = min(n_chunk, n - c)
        o_ref[:, c:c + w] = jnp.dot(xn, w_ref[:, c:c + w], preferred_element_type=F32).astype(o_ref.dtype)


def rms_matmul(x, gain, w, *, tm, out_dtype=F32):
    m, d = x.shape
    n = w.shape[1]
    return pl.pallas_call(
        functools.partial(_rms_matmul_kernel, n_chunk=512),
        grid=(m // tm,),
        in_specs=[pl.BlockSpec((tm, d), lambda i: (i, 0)),
                  pl.BlockSpec((1, d), lambda i: (0, 0)),
                  pl.BlockSpec((d, n), lambda i: (0, 0))],
        out_specs=pl.BlockSpec((tm, n), lambda i: (i, 0)),
        out_shape=jax.ShapeDtypeStruct((m, n), out_dtype),
        compiler_params=_cparams(("parallel",)),
        name="rms_matmul",
    )(x, gain.reshape(1, d), w)


def _matmul_residual_kernel(a_ref, w_ref, h_ref, o_ref):
    o_ref[...] = h_ref[...] + jnp.dot(a_ref[...], w_ref[...], preferred_element_type=F32)


def matmul_residual(a, w, h, *, tm):
    m, k = a.shape
    n = w.shape[1]
    return pl.pallas_call(
        _matmul_residual_kernel,
        grid=(m // tm,),
        in_specs=[pl.BlockSpec((tm, k), lambda i: (i, 0)),
                  pl.BlockSpec((k, n), lambda i: (0, 0)),
                  pl.BlockSpec((tm, n), lambda i: (i, 0))],
        out_specs=pl.BlockSpec((tm, n), lambda i: (i, 0)),
        out_shape=jax.ShapeDtypeStruct((m, n), F32),
        compiler_params=_cparams(("parallel",)),
        name="matmul_residual",
    )(a, w, h)


def _conv_ffn_kernel(h_ref, halo_ref, g_ref, wup_ref, cw_ref, wd_ref, o_ref,
                     xn_ref, ug_ref, uv_ref, acc_ref, *, tiles_per_seq, fc):
    tm = h_ref.shape[0]
    hb = halo_ref.shape[0]
    gain = g_ref[...]
    first = (pl.program_id(0) % tiles_per_seq) == 0
    halo = _rms(halo_ref[...], gain)
    xn_ref[0:hb, :] = jnp.where(first, 0.0, halo).astype(BF16)
    xn_ref[hb:, :] = _rms(h_ref[...], gain).astype(BF16)
    xn = xn_ref[...]

    def conv(u_ref, col):
        w = cw_ref[:, col:col + fc]
        y = w[0:1, :] * u_ref[hb - 2:hb - 2 + tm, :]
        y = y + w[1:2, :] * u_ref[hb - 1:hb - 1 + tm, :]
        return y + w[2:3, :] * u_ref[hb:hb + tm, :]

    for c in range(D_FF // fc):
        cg, cv = c * fc, D_FF + c * fc
        ug_ref[...] = jnp.dot(xn, wup_ref[:, cg:cg + fc], preferred_element_type=F32)
        uv_ref[...] = jnp.dot(xn, wup_ref[:, cv:cv + fc], preferred_element_type=F32)
        act = (_silu(conv(ug_ref, cg)) * conv(uv_ref, cv)).astype(BF16)
        part = jnp.dot(act, wd_ref[cg:cg + fc, :], preferred_element_type=F32)
        if c == 0:
            acc_ref[...] = part
        else:
            acc_ref[...] += part
    o_ref[...] = h_ref[...] + acc_ref[...]


def conv_ffn(h, gain, w_up, conv_w, w_down, *, tm, seq):
    m, d = h.shape
    hb = BF16_ROWS
    fc = 256
    tiles_per_seq = seq // tm
    return pl.pallas_call(
        functools.partial(_conv_ffn_kernel, tiles_per_seq=tiles_per_seq, fc=fc),
        grid=(m // tm,),
        in_specs=[pl.BlockSpec((tm, d), lambda i: (i, 0)),
                  pl.BlockSpec((hb, d), lambda i: (jnp.maximum(i * (tm // hb) - 1, 0), 0)),
                  pl.BlockSpec((1, d), lambda i: (0, 0)),
                  pl.BlockSpec((d, 2 * D_FF), lambda i: (0, 0), pipeline_mode=pl.Buffered(1)),
                  pl.BlockSpec((FFN_CONV, 2 * D_FF), lambda i: (0, 0)),
                  pl.BlockSpec((D_FF, d), lambda i: (0, 0), pipeline_mode=pl.Buffered(1))],
        out_specs=pl.BlockSpec((tm, d), lambda i: (i, 0)),
        out_shape=jax.ShapeDtypeStruct((m, d), F32),
        scratch_shapes=[pltpu.VMEM((tm + hb, d), BF16),
                        pltpu.VMEM((tm + hb, fc), F32),
                        pltpu.VMEM((tm + hb, fc), F32),
                        pltpu.VMEM((tm, d), F32)],
        compiler_params=_cparams(("parallel",)),
        name="conv_ffn",
    )(h, h, gain.reshape(1, d), w_up, conv_w, w_down)


def _ple_kernel(h_ref, p_ref, g_ref, wg_ref, wp_ref, gf_ref, o_ref, *, final):
    h = h_ref[...]
    xn = _rms(h, g_ref[...]).astype(BF16)
    gate = _sigmoid(jnp.dot(xn, wg_ref[...], preferred_element_type=F32))
    emb = jnp.dot(p_ref[...].astype(BF16), wp_ref[...], preferred_element_type=F32)
    out = h + gate * emb
    if final:
        out = _rms(out, gf_ref[...])
    o_ref[...] = out


def ple(h, p, gain, w_gate, w_proj, gain_final, *, tm, final):
    m, d = h.shape
    pd = p.shape[1]
    return pl.pallas_call(
        functools.partial(_ple_kernel, final=final),
        grid=(m // tm,),
        in_specs=[pl.BlockSpec((tm, d), lambda i: (i, 0)),
                  pl.BlockSpec((tm, pd), lambda i: (i, 0)),
                  pl.BlockSpec((1, d), lambda i: (0, 0)),
                  pl.BlockSpec((d, d), lambda i: (0, 0)),
                  pl.BlockSpec((pd, d), lambda i: (0, 0)),
                  pl.BlockSpec((1, d), lambda i: (0, 0))],
        out_specs=pl.BlockSpec((tm, d), lambda i: (i, 0)),
        out_shape=jax.ShapeDtypeStruct((m, d), F32),
        compiler_params=_cparams(("parallel",)),
        name="ple",
    )(h, p, gain.reshape(1, d), w_gate, w_proj, gain_final.reshape(1, d))


def _block_diag(x):
    lane = lax.broadcasted_iota(jnp.int32, x.shape, 1)
    top = jnp.where(lane < LANES, x, 0.0)
    bot = jnp.where(lane >= LANES, x, 0.0)
    return jnp.concatenate([top, bot], axis=0).astype(BF16)


def _pair_dot(a, b_bd):
    return jnp.dot(a.astype(BF16), b_bd, preferred_element_type=F32)


def _delta_kernel(q_ref, k_ref, v_ref, z_ref, ba_ref, cw_ref, al_ref, dtb_ref, nw_ref, o_ref,
                  ext_ref, qn_ref, kn_ref, vn_ref, gcol_ref, grow_ref, s_ref, *, chunk):
    tt = q_ref.shape[0]
    hw = 2 * HEAD_DIM_A
    hg = pl.program_id(1)
    tstep = pl.program_id(2)
    pad = 8
    n_chunks = tt // chunk

    @pl.when(tstep == 0)
    def _():
        ext_ref[0:pad, :] = jnp.zeros((pad, 3 * hw), F32)
        s_ref[...] = jnp.zeros(s_ref.shape, F32)

    ext_ref[pad:, 0:hw] = q_ref[...]
    ext_ref[pad:, hw:2 * hw] = k_ref[...]
    ext_ref[pad:, 2 * hw:3 * hw] = v_ref[...]
    cw = cw_ref[0]
    for part, dst in ((0, qn_ref), (1, kn_ref), (2, vn_ref)):
        lo = part * hw
        y = cw[part, 0:1, :] * ext_ref[pad - 3:pad - 3 + tt, lo:lo + hw]
        for i in range(1, CONV_A):
            y = y + cw[part, i:i + 1, :] * ext_ref[pad - 3 + i:pad - 3 + i + tt, lo:lo + hw]
        y = _silu(y)
        if part < 2:
            for hh in range(2):
                yh = y[:, hh * HEAD_DIM_A:(hh + 1) * HEAD_DIM_A]
                yh = yh * lax.rsqrt(jnp.sum(yh * yh, axis=-1, keepdims=True) + EPS)
                if part == 0:
                    yh = yh * (HEAD_DIM_A ** -0.5)
                dst[:, hh * HEAD_DIM_A:(hh + 1) * HEAD_DIM_A] = yh
        else:
            dst[...] = y
    ext_ref[0:pad, :] = ext_ref[tt:tt + pad, :]

    ba = ba_ref[...]
    lane = lax.broadcasted_iota(jnp.int32, ba.shape, 1)
    sp = ba + dtb_ref[...]
    softplus = jnp.maximum(sp, 0.0) + jnp.log1p(jnp.exp(-jnp.abs(sp)))
    gb = jnp.where(lane < N_HEADS_A, _sigmoid(ba), -jnp.exp(al_ref[...]) * softplus)
    gcol_ref[:, 0:LANES] = gb
    grow_ref[...] = gb.T
    g8 = grow_ref[N_HEADS_A:2 * N_HEADS_A, :]
    tl = lax.broadcasted_iota(jnp.int32, g8.shape, 1) % chunk
    sh = 1
    while sh < chunk:
        g8 = g8 + jnp.where(tl >= sh, pltpu.roll(g8, sh, 1), 0.0)
        sh *= 2
    grow_ref[N_HEADS_A:2 * N_HEADS_A, :] = g8
    gcol_ref[:, LANES:2 * LANES] = grow_ref[...].T

    h_a = hg * 2
    ii = lax.broadcasted_iota(jnp.int32, (chunk, hw), 0)
    lane2 = lax.broadcasted_iota(jnp.int32, (chunk, hw), 1)
    jj = lane2 % LANES
    first = lane2 < LANES
    first_row = lax.broadcasted_iota(jnp.int32, (1, hw), 1) < LANES
    lane_c = lax.broadcasted_iota(jnp.int32, (chunk, LANES), 1)
    row_h = lax.broadcasted_iota(jnp.int32, (N_HEADS_A, chunk), 0)
    incl = ii >= jj
    strict = ii > jj
    eye2 = (ii == jj).astype(F32)
    nw = nw_ref[...]

    def col_pair(x, lane_a):
        a = jnp.sum(jnp.where(lane_c == lane_a, x, 0.0), axis=1, keepdims=True)
        b = jnp.sum(jnp.where(lane_c == lane_a + 1, x, 0.0), axis=1, keepdims=True)
        return jnp.where(first, a, b)

    loc = []
    for c in range(n_chunks):
        rows = slice(c * chunk, (c + 1) * chunk)
        q2, k2, v2 = qn_ref[rows, :], kn_ref[rows, :], vn_ref[rows, :]
        beta2 = col_pair(gcol_ref[rows, 0:LANES], h_a)
        gc2 = col_pair(gcol_ref[rows, LANES:2 * LANES], N_HEADS_A + h_a)
        grc = grow_ref[N_HEADS_A:2 * N_HEADS_A, rows]
        gr_a = jnp.sum(jnp.where(row_h == h_a, grc, 0.0), axis=0, keepdims=True)
        gr_b = jnp.sum(jnp.where(row_h == h_a + 1, grc, 0.0), axis=0, keepdims=True)
        gr2 = jnp.concatenate([gr_a, gr_b], axis=1)
        gl2 = jnp.where(first_row, gr_a[:, chunk - 1:chunk], gr_b[:, chunk - 1:chunk])
        decay2 = jnp.where(incl, jnp.exp(gc2 - gr2), 0.0)
        eg2 = jnp.exp(gc2)
        kq = lax.dot_general(jnp.concatenate([k2, q2], axis=0).astype(BF16), _block_diag(k2),
                             (((1,), (1,)), ((), ())), preferred_element_type=F32)
        lmat = jnp.where(strict, beta2 * kq[:chunk] * decay2, 0.0)
        loc.append(dict(
            lmat=lmat, attn=(kq[chunk:] * decay2).astype(BF16),
            bv=_block_diag(beta2 * v2), bk=_block_diag((beta2 * eg2) * k2),
            qd=(q2 * eg2).astype(BF16), kd=(k2 * jnp.exp(gl2 - gc2)).astype(BF16),
            egl=jnp.exp(gl2)))

    ts = None
    b = 1
    while b < chunk:
        sel = ((ii // (2 * b)) == (jj // (2 * b))) & (((ii // b) % 2) == 1) & (((jj // b) % 2) == 0)
        xs = [jnp.where(sel, l["lmat"], 0.0) for l in loc]
        if ts is None:
            ts = [eye2 - x for x in xs]
        else:
            xts = [_pair_dot(x, _block_diag(t)) for x, t in zip(xs, ts)]
            ts = [t - _pair_dot(t, _block_diag(xt)) for t, xt in zip(ts, xts)]
        b *= 2
    for l, t in zip(loc, ts):
        tb = t.astype(BF16)
        l["u0"] = jnp.dot(tb, l["bv"], preferred_element_type=F32)
        l["wq"] = jnp.concatenate(
            [jnp.dot(tb, l["bk"], preferred_element_type=F32).astype(BF16), l["qd"]], axis=0)

    s2 = s_ref[...]
    for c, l in enumerate(loc):
        rows = slice(c * chunk, (c + 1) * chunk)
        ws = jnp.dot(l["wq"], _block_diag(s2), preferred_element_type=F32)
        u_bd = _block_diag(l["u0"] - ws[:chunk])
        o2 = ws[chunk:] + jnp.dot(l["attn"], u_bd, preferred_element_type=F32)
        kd_rows = jnp.concatenate([l["kd"][:, :LANES], l["kd"][:, LANES:]], axis=0)
        s2 = l["egl"] * s2 + lax.dot_general(kd_rows, u_bd, (((0,), (0,)), ((), ())),
                                             preferred_element_type=F32)
        for hh in range(2):
            cs = slice(hh * HEAD_DIM_A, (hh + 1) * HEAD_DIM_A)
            o_ref[rows, cs] = (_rms(o2[:, cs], nw) * _silu(z_ref[rows, cs])).astype(o_ref.dtype)
    s_ref[...] = s2


def delta_mixer(proj, conv_w, a_log, dt_bias, norm_w, *, batch, seq):
    heads, chunk = DELTA_HEADS, DELTA_CHUNK
    tt = min(DELTA_TILE, seq)
    hk = N_HEADS_A * HEAD_DIM_A
    hw = heads * HEAD_DIM_A
    n_hg = N_HEADS_A // heads
    n_t = seq // tt
    m = batch * seq
    cw = conv_w.reshape(CONV_A, 3, n_hg, hw).transpose(2, 1, 0, 3)
    pad_row = jnp.zeros((1, LANES), F32)
    al = pad_row.at[0, N_HEADS_A:2 * N_HEADS_A].set(a_log.astype(F32))
    dtb = pad_row.at[0, N_HEADS_A:2 * N_HEADS_A].set(dt_bias.astype(F32))

    def col(off):
        return lambda b, g, t: (b * n_t + t, off * n_hg + g)

    return pl.pallas_call(
        functools.partial(_delta_kernel, chunk=chunk),
        grid=(batch, n_hg, n_t),
        in_specs=[pl.BlockSpec((tt, hw), col(0)),
                  pl.BlockSpec((tt, hw), col(1)),
                  pl.BlockSpec((tt, hw), col(2)),
                  pl.BlockSpec((tt, hw), col(3)),
                  pl.BlockSpec((tt, LANES), lambda b, g, t: (b * n_t + t, 4 * hk // LANES)),
                  pl.BlockSpec((1, 3, CONV_A, hw), lambda b, g, t: (g, 0, 0, 0)),
                  pl.BlockSpec((1, LANES), lambda b, g, t: (0, 0)),
                  pl.BlockSpec((1, LANES), lambda b, g, t: (0, 0)),
                  pl.BlockSpec((1, HEAD_DIM_A), lambda b, g, t: (0, 0))],
        out_specs=pl.BlockSpec((tt, hw), lambda b, g, t: (b * n_t + t, g)),
        out_shape=jax.ShapeDtypeStruct((m, hk), BF16),
        scratch_shapes=[pltpu.VMEM((tt + 8, 3 * hw), F32),
                        pltpu.VMEM((tt, hw), F32),
                        pltpu.VMEM((tt, hw), F32),
                        pltpu.VMEM((tt, hw), F32),
                        pltpu.VMEM((tt, 2 * LANES), F32),
                        pltpu.VMEM((LANES, tt), F32),
                        pltpu.VMEM((HEAD_DIM_A, hw), F32)],
        compiler_params=_cparams(("parallel", "parallel", "arbitrary")),
        name="delta_mixer",
    )(proj, proj, proj, proj, proj, cw, al, dtb, norm_w.reshape(1, HEAD_DIM_A))


def _swa_kernel(sink_ref, q_ref, kc_ref, vc_ref, kp_ref, vp_ref, o_ref, *, n_blk):
    w = WINDOW
    grp = N_HEADS_B // N_KV_B
    hd = HEAD_DIM_B
    blk = pl.program_id(0) % n_blk
    qi = lax.broadcasted_iota(jnp.int32, (w, 2 * w), 0)
    kj = lax.broadcasted_iota(jnp.int32, (w, 2 * w), 1)
    dist = qi + w - kj
    valid = (dist >= 0) & (dist < w) & ((kj >= w) | (blk > 0))
    distf = dist.astype(F32)
    scale = hd ** -0.5
    for h in range(N_KV_B):
        kb = jnp.concatenate([kp_ref[:, h * hd:(h + 1) * hd], kc_ref[:, h * hd:(h + 1) * hd]], axis=0)
        vb = jnp.concatenate([vp_ref[:, h * hd:(h + 1) * hd], vc_ref[:, h * hd:(h + 1) * hd]], axis=0)
        qs = jnp.concatenate([q_ref[:, (h * grp + g) * hd:(h * grp + g + 1) * hd] for g in range(grp)], axis=0)
        s = lax.dot_general(qs.astype(BF16), kb.astype(BF16), (((1,), (1,)), ((), ())),
                            preferred_element_type=F32) * scale
        es = []
        rs = []
        for g in range(grp):
            head = h * grp + g
            slope = 2.0 ** (-8.0 * (head + 1) / N_HEADS_B)
            sink = sink_ref[head]
            logits = jnp.where(valid, s[g * w:(g + 1) * w] - slope * distf, -jnp.inf)
            mx = jnp.maximum(jnp.max(logits, axis=-1, keepdims=True), sink)
            e = jnp.exp(logits - mx)
            den = jnp.sum(e, axis=-1, keepdims=True) + jnp.exp(sink - mx)
            es.append(e.astype(BF16))
            rs.append(1.0 / den)
        pv = jnp.dot(jnp.concatenate(es, axis=0), vb.astype(BF16), preferred_element_type=F32)
        for g in range(grp):
            head = h * grp + g
            o_ref[:, head * hd:(head + 1) * hd] = (pv[g * w:(g + 1) * w] * rs[g]).astype(o_ref.dtype)


def swa_mixer(proj, sinks, *, batch, seq):
    w = WINDOW
    n_blk = seq // w
    qd = N_HEADS_B * HEAD_DIM_B
    kd = N_KV_B * HEAD_DIM_B
    m = batch * seq
    kcol, vcol = qd // kd, qd // kd + 1
    prev = lambda i: jnp.maximum(i - 1, 0)
    return pl.pallas_call(
        functools.partial(_swa_kernel, n_blk=n_blk),
        grid=(m // w,),
        in_specs=[pl.BlockSpec(memory_space=pltpu.SMEM),
                  pl.BlockSpec((w, qd), lambda i: (i, 0)),
                  pl.BlockSpec((w, kd), lambda i: (i, kcol)),
                  pl.BlockSpec((w, kd), lambda i: (i, vcol)),
                  pl.BlockSpec((w, kd), lambda i: (prev(i), kcol)),
                  pl.BlockSpec((w, kd), lambda i: (prev(i), vcol))],
        out_specs=pl.BlockSpec((w, qd), lambda i: (i, 0)),
        out_shape=jax.ShapeDtypeStruct((m, qd), BF16),
        compiler_params=_cparams(("parallel",)),
        name="swa_mixer",
    )(sinks.astype(F32), proj, proj, proj, proj, proj)


def kernel(x, p, norm_mix, norm_ffn, norm_ple, norm_final, a_w_in, a_conv, a_log, a_dt_bias, a_norm, a_w_out,
           b_w_in, b_sinks, b_w_out, f_w_up, f_conv, f_w_down, ple_w_proj, ple_w_gate):
    batch, seq, d = x.shape
    depth = p.shape[0]
    m = batch * seq
    hk = N_HEADS_A * HEAD_DIM_A
    h = x.reshape(m, d)
    for i in range(depth):
        j = i // 2
        if i % 2 == 0:
            w_in = a_w_in[j]
            w_in = jnp.pad(w_in, ((0, 0), (0, 4 * hk + LANES - w_in.shape[1]))).astype(BF16)
            proj = rms_matmul(h, norm_mix[i], w_in, tm=256)
            mix = delta_mixer(proj, a_conv[j], a_log[j], a_dt_bias[j], a_norm[j], batch=batch, seq=seq)
            w_out = a_w_out[j]
        else:
            proj = rms_matmul(h, norm_mix[i], b_w_in[j].astype(BF16), tm=512)
            mix = swa_mixer(proj, b_sinks[j], batch=batch, seq=seq)
            w_out = b_w_out[j]
        h = matmul_residual(mix, w_out.astype(BF16), h, tm=512)
        h = conv_ffn(h, norm_ffn[i], f_w_up[i].astype(BF16), f_conv[i], f_w_down[i].astype(BF16),
                     tm=512, seq=seq)
        h = ple(h, p[i].reshape(m, -1), norm_ple[i], ple_w_gate[i].astype(BF16), ple_w_proj[i].astype(BF16),
                norm_final, tm=512, final=(i == depth - 1))
    return h.reshape(batch, seq, d)
```

```python
import functools
import math

import jax
import jax.numpy as jnp
from jax import lax
from jax.experimental import pallas as pl
from jax.experimental.pallas import tpu as pltpu

F32 = jnp.float32
BF16 = jnp.bfloat16

D_MODEL = 1024
N_HEADS_A = 8
HEAD_DIM_A = 128
CONV_A = 4
N_HEADS_B = 16
N_KV_B = 4
HEAD_DIM_B = 64
WINDOW = 128
D_FF = 2816
FFN_CONV = 3
PLE_DIM = 256
EPS = 1e-6

LANES = 128
BF16_ROWS = 16
VMEM_LIMIT = 56 * 1024 * 1024

DELTA_CHUNK = 128
DELTA_TILE = 1024
DELTA_HEADS = 2


def _cparams(sem):
    return pltpu.CompilerParams(dimension_semantics=sem, vmem_limit_bytes=VMEM_LIMIT)


def _rms(x, gain):
    ms = jnp.mean(x * x, axis=-1, keepdims=True)
    return x * lax.rsqrt(ms + EPS) * gain


def _sigmoid(x):
    return 0.5 + 0.5 * jnp.tanh(0.5 * x)


def _silu(x):
    h = 0.5 * x
    return h + h * jnp.tanh(h)


def _rms_matmul_kernel(x_ref, g_ref, w_ref, o_ref, *, n_chunk):
    xn = _rms(x_ref[...], g_ref[...]).astype(BF16)
    n = o_ref.shape[1]
    for c in range(0, n, n_chunk):
        w = min(n_chunk, n - c)
        o_ref[:, c:c + w] = jnp.dot(xn, w_ref[:, c:c + w], preferred_element_type=F32).astype(o_ref.dtype)


def rms_matmul(x, gain, w, *, tm, out_dtype=F32):
    m, d = x.shape
    n = w.shape[1]
    return pl.pallas_call(
        functools.partial(_rms_matmul_kernel, n_chunk=512),
        grid=(m // tm,),
        in_specs=[pl.BlockSpec((tm, d), lambda i: (i, 0)),
                  pl.BlockSpec((1, d), lambda i: (0, 0)),
                  pl.BlockSpec((d, n), lambda i: (0, 0))],
        out_specs=pl.BlockSpec((tm, n), lambda i: (i, 0)),
        out_shape=jax.ShapeDtypeStruct((m, n), out_dtype),
        compiler_params=_cparams(("parallel",)),
        name="rms_matmul",
    )(x, gain.reshape(1, d), w)


def _matmul_residual_kernel(a_ref, w_ref, h_ref, o_ref):
    o_ref[...] = h_ref[...] + jnp.dot(a_ref[...], w_ref[...], preferred_element_type=F32)


def matmul_residual(a, w, h, *, tm):
    m, k = a.shape
    n = w.shape[1]
    return pl.pallas_call(
        _matmul_residual_kernel,
        grid=(m // tm,),
        in_specs=[pl.BlockSpec((tm, k), lambda i: (i, 0)),
                  pl.BlockSpec((k, n), lambda i: (0, 0)),
                  pl.BlockSpec((tm, n), lambda i: (i, 0))],
        out_specs=pl.BlockSpec((tm, n), lambda i: (i, 0)),
        out_shape=jax.ShapeDtypeStruct((m, n), F32),
        compiler_params=_cparams(("parallel",)),
        name="matmul_residual",
    )(a, w, h)


def _conv_ffn_kernel(h_ref, halo_ref, g_ref, wup_ref, cw_ref, wd_ref, o_ref,
                     xn_ref, acc_ref, *, tiles_per_seq, fc):
    hb = halo_ref.shape[0]
    gain = g_ref[...]
    first = (pl.program_id(0) % tiles_per_seq) == 0
    halo = _rms(halo_ref[...], gain)
    xn_ref[0:hb, :] = jnp.where(first, 0.0, halo).astype(BF16)
    xn_ref[hb:, :] = _rms(h_ref[...], gain).astype(BF16)
    xn = xn_ref[...]

    def conv(col):
        u = jnp.dot(xn, wup_ref[:, col:col + fc], preferred_element_type=F32)
        w = cw_ref[:, col:col + fc]
        y = w[2:3, :] * u + w[1:2, :] * pltpu.roll(u, 1, 0) + w[0:1, :] * pltpu.roll(u, 2, 0)
        return y[hb:]

    n_c = D_FF // fc
    nxt = (conv(0), conv(D_FF))
    for c in range(n_c):
        cg = c * fc
        gate, val = nxt
        if c + 1 < n_c:
            nxt = (conv(cg + fc), conv(D_FF + cg + fc))
        act = (_silu(gate) * val).astype(BF16)
        part = jnp.dot(act, wd_ref[cg:cg + fc, :], preferred_element_type=F32)
        if c == 0:
            acc_ref[...] = part
        else:
            acc_ref[...] += part
    o_ref[...] = h_ref[...] + acc_ref[...]


def conv_ffn(h, gain, w_up, conv_w, w_down, *, tm, seq):
    m, d = h.shape
    hb = BF16_ROWS
    fc = 256
    tiles_per_seq = seq // tm
    return pl.pallas_call(
        functools.partial(_conv_ffn_kernel, tiles_per_seq=tiles_per_seq, fc=fc),
        grid=(m // tm,),
        in_specs=[pl.BlockSpec((tm, d), lambda i: (i, 0)),
                  pl.BlockSpec((hb, d), lambda i: (jnp.maximum(i * (tm // hb) - 1, 0), 0)),
                  pl.BlockSpec((1, d), lambda i: (0, 0)),
                  pl.BlockSpec((d, 2 * D_FF), lambda i: (0, 0), pipeline_mode=pl.Buffered(1)),
                  pl.BlockSpec((FFN_CONV, 2 * D_FF), lambda i: (0, 0)),
                  pl.BlockSpec((D_FF, d), lambda i: (0, 0), pipeline_mode=pl.Buffered(1))],
        out_specs=pl.BlockSpec((tm, d), lambda i: (i, 0)),
        out_shape=jax.ShapeDtypeStruct((m, d), F32),
        scratch_shapes=[pltpu.VMEM((tm + hb, d), BF16),
                        pltpu.VMEM((tm, d), F32)],
        compiler_params=_cparams(("parallel",)),
        name="conv_ffn",
    )(h, h, gain.reshape(1, d), w_up, conv_w, w_down)


def _ple_kernel(h_ref, p_ref, g_ref, wg_ref, wp_ref, gf_ref, o_ref, *, final):
    h = h_ref[...]
    xn = _rms(h, g_ref[...]).astype(BF16)
    gate = _sigmoid(jnp.dot(xn, wg_ref[...], preferred_element_type=F32))
    emb = jnp.dot(p_ref[...].astype(BF16), wp_ref[...], preferred_element_type=F32)
    out = h + gate * emb
    if final:
        out = _rms(out, gf_ref[...])
    o_ref[...] = out


def ple(h, p, gain, w_gate, w_proj, gain_final, *, tm, final):
    m, d = h.shape
    pd = p.shape[1]
    return pl.pallas_call(
        functools.partial(_ple_kernel, final=final),
        grid=(m // tm,),
        in_specs=[pl.BlockSpec((tm, d), lambda i: (i, 0)),
                  pl.BlockSpec((tm, pd), lambda i: (i, 0)),
                  pl.BlockSpec((1, d), lambda i: (0, 0)),
                  pl.BlockSpec((d, d), lambda i: (0, 0)),
                  pl.BlockSpec((pd, d), lambda i: (0, 0)),
                  pl.BlockSpec((1, d), lambda i: (0, 0))],
        out_specs=pl.BlockSpec((tm, d), lambda i: (i, 0)),
        out_shape=jax.ShapeDtypeStruct((m, d), F32),
        compiler_params=_cparams(("parallel",)),
        name="ple",
    )(h, p, gain.reshape(1, d), w_gate, w_proj, gain_final.reshape(1, d))


def _block_diag(x):
    xb = x.astype(BF16)
    z = jnp.zeros((x.shape[0], LANES), BF16)
    top = jnp.concatenate([xb[:, :LANES], z], axis=1)
    bot = jnp.concatenate([z, xb[:, LANES:]], axis=1)
    return jnp.concatenate([top, bot], axis=0)


def _pair_dot(a, b):
    return jnp.dot(a.astype(BF16), _block_diag(b), preferred_element_type=F32)


def _delta_kernel(q_ref, k_ref, v_ref, z_ref, ba_ref, cw_ref, al_ref, dtb_ref, nw_ref, o_ref,
                  ext_ref, qn_ref, kn_ref, vn_ref, gcol_ref, grow_ref, s_ref, *, chunk):
    tt = q_ref.shape[0]
    hw = 2 * HEAD_DIM_A
    hg = pl.program_id(1)
    tstep = pl.program_id(2)
    pad = 8
    n_chunks = tt // chunk

    @pl.when(tstep == 0)
    def _():
        ext_ref[0:pad, :] = jnp.zeros((pad, 3 * hw), F32)
        s_ref[...] = jnp.zeros(s_ref.shape, F32)

    ext_ref[pad:, 0:hw] = q_ref[...]
    ext_ref[pad:, hw:2 * hw] = k_ref[...]
    ext_ref[pad:, 2 * hw:3 * hw] = v_ref[...]
    cw = cw_ref[0]
    for part, dst in ((0, qn_ref), (1, kn_ref), (2, vn_ref)):
        lo = part * hw
        xe = ext_ref[:, lo:lo + hw]
        y = cw[part, CONV_A - 1:CONV_A, :] * xe
        for i in range(CONV_A - 1):
            y = y + cw[part, i:i + 1, :] * pltpu.roll(xe, CONV_A - 1 - i, 0)
        y = _silu(y[pad:])
        if part < 2:
            for hh in range(2):
                yh = y[:, hh * HEAD_DIM_A:(hh + 1) * HEAD_DIM_A]
                yh = yh * lax.rsqrt(jnp.sum(yh * yh, axis=-1, keepdims=True) + EPS)
                if part == 0:
                    yh = yh * (HEAD_DIM_A ** -0.5)
                dst[:, hh * HEAD_DIM_A:(hh + 1) * HEAD_DIM_A] = yh
        else:
            dst[...] = y
    ext_ref[0:pad, :] = ext_ref[tt:tt + pad, :]

    ba = ba_ref[...]
    lane = lax.broadcasted_iota(jnp.int32, ba.shape, 1)
    sp = ba + dtb_ref[...]
    softplus = jnp.maximum(sp, 0.0) + jnp.log1p(jnp.exp(-jnp.abs(sp)))
    gb = jnp.where(lane < N_HEADS_A, _sigmoid(ba), -jnp.exp(al_ref[...]) * softplus)
    gcol_ref[:, 0:LANES] = gb
    grow_ref[...] = gb.T
    g8 = grow_ref[N_HEADS_A:2 * N_HEADS_A, :]
    tl = lax.broadcasted_iota(jnp.int32, g8.shape, 1) % chunk
    sh = 1
    while sh < chunk:
        g8 = g8 + jnp.where(tl >= sh, pltpu.roll(g8, sh, 1), 0.0)
        sh *= 2
    grow_ref[N_HEADS_A:2 * N_HEADS_A, :] = g8
    gcol_ref[:, LANES:2 * LANES] = grow_ref[...].T

    h_a = hg * 2
    ii = lax.broadcasted_iota(jnp.int32, (chunk, hw), 0)
    lane2 = lax.broadcasted_iota(jnp.int32, (chunk, hw), 1)
    jj = lane2 % LANES
    first = lane2 < LANES
    first_row = lax.broadcasted_iota(jnp.int32, (1, hw), 1) < LANES
    lane_c = lax.broadcasted_iota(jnp.int32, (chunk, LANES), 1)
    row_h = lax.broadcasted_iota(jnp.int32, (N_HEADS_A, chunk), 0)
    incl = ii >= jj
    strict = ii > jj
    ixj = ii ^ jj
    eye2 = (ii == jj).astype(F32)
    nw = nw_ref[...]

    def col_pair(x, lane_a):
        a = jnp.sum(jnp.where(lane_c == lane_a, x, 0.0), axis=1, keepdims=True)
        b = jnp.sum(jnp.where(lane_c == lane_a + 1, x, 0.0), axis=1, keepdims=True)
        return jnp.where(first, a, b)

    loc = []
    for c in range(n_chunks):
        rows = slice(c * chunk, (c + 1) * chunk)
        q2, k2, v2 = qn_ref[rows, :], kn_ref[rows, :], vn_ref[rows, :]
        beta2 = col_pair(gcol_ref[rows, 0:LANES], h_a)
        gc2 = col_pair(gcol_ref[rows, LANES:2 * LANES], N_HEADS_A + h_a)
        grc = grow_ref[N_HEADS_A:2 * N_HEADS_A, rows]
        gr_a = jnp.sum(jnp.where(row_h == h_a, grc, 0.0), axis=0, keepdims=True)
        gr_b = jnp.sum(jnp.where(row_h == h_a + 1, grc, 0.0), axis=0, keepdims=True)
        gr2 = jnp.concatenate([gr_a, gr_b], axis=1)
        gl2 = jnp.where(first_row, gr_a[:, chunk - 1:chunk], gr_b[:, chunk - 1:chunk])
        decay2 = jnp.where(incl, jnp.exp(gc2 - gr2), 0.0)
        eg2 = jnp.exp(gc2)
        k2b = k2.astype(BF16)
        kq = lax.dot_general(jnp.concatenate([k2b, q2.astype(BF16)], axis=0), _block_diag(k2b),
                             (((1,), (1,)), ((), ())), preferred_element_type=F32)
        lmat = jnp.where(strict, beta2 * kq[:chunk] * decay2, 0.0)
        loc.append(dict(
            lmat=lmat, attn=(kq[chunk:] * decay2).astype(BF16),
            bv=_block_diag(beta2 * v2), bk=_block_diag((beta2 * eg2) * k2),
            qd=(q2 * eg2).astype(BF16), kd=(k2 * jnp.exp(gl2 - gc2)).astype(BF16),
            egl=jnp.exp(gl2)))

    ts = None
    b = 1
    while b < chunk:
        sel = (ixj >= b) & (ixj < 2 * b)
        xs = [jnp.where(sel, l["lmat"], 0.0) for l in loc]
        if ts is None:
            ts = [eye2 - x for x in xs]
        else:
            tbs = [t.astype(BF16) for t in ts]
            xts = [_pair_dot(x, tb) for x, tb in zip(xs, tbs)]
            ts = [t - _pair_dot(tb, xt) for t, tb, xt in zip(ts, tbs, xts)]
        b *= 2
    for l, t in zip(loc, ts):
        tb = t.astype(BF16)
        l["u0"] = jnp.dot(tb, l["bv"], preferred_element_type=F32)
        l["wq"] = jnp.concatenate(
            [jnp.dot(tb, l["bk"], preferred_element_type=F32).astype(BF16), l["qd"]], axis=0)

    s2 = s_ref[...]
    for c, l in enumerate(loc):
        rows = slice(c * chunk, (c + 1) * chunk)
        ws = jnp.dot(l["wq"], _block_diag(s2), preferred_element_type=F32)
        u_bd = _block_diag(l["u0"] - ws[:chunk])
        o2 = ws[chunk:] + jnp.dot(l["attn"], u_bd, preferred_element_type=F32)
        kd_rows = jnp.concatenate([l["kd"][:, :LANES], l["kd"][:, LANES:]], axis=0)
        s2 = l["egl"] * s2 + lax.dot_general(kd_rows, u_bd, (((0,), (0,)), ((), ())),
                                             preferred_element_type=F32)
        for hh in range(2):
            cs = slice(hh * HEAD_DIM_A, (hh + 1) * HEAD_DIM_A)
            o_ref[rows, cs] = (_rms(o2[:, cs], nw) * _silu(z_ref[rows, cs])).astype(o_ref.dtype)
    s_ref[...] = s2


def delta_mixer(proj, conv_w, a_log, dt_bias, norm_w, *, batch, seq):
    heads, chunk = DELTA_HEADS, DELTA_CHUNK
    tt = min(DELTA_TILE, seq)
    hk = N_HEADS_A * HEAD_DIM_A
    hw = heads * HEAD_DIM_A
    n_hg = N_HEADS_A // heads
    n_t = seq // tt
    m = batch * seq
    cw = conv_w.reshape(CONV_A, 3, n_hg, hw).transpose(2, 1, 0, 3)
    pad_row = jnp.zeros((1, LANES), F32)
    al = pad_row.at[0, N_HEADS_A:2 * N_HEADS_A].set(a_log.astype(F32))
    dtb = pad_row.at[0, N_HEADS_A:2 * N_HEADS_A].set(dt_bias.astype(F32))

    def col(off):
        return lambda b, g, t: (b * n_t + t, off * n_hg + g)

    return pl.pallas_call(
        functools.partial(_delta_kernel, chunk=chunk),
        grid=(batch, n_hg, n_t),
        in_specs=[pl.BlockSpec((tt, hw), col(0)),
                  pl.BlockSpec((tt, hw), col(1)),
                  pl.BlockSpec((tt, hw), col(2)),
                  pl.BlockSpec((tt, hw), col(3)),
                  pl.BlockSpec((tt, LANES), lambda b, g, t: (b * n_t + t, 4 * hk // LANES)),
                  pl.BlockSpec((1, 3, CONV_A, hw), lambda b, g, t: (g, 0, 0, 0)),
                  pl.BlockSpec((1, LANES), lambda b, g, t: (0, 0)),
                  pl.BlockSpec((1, LANES), lambda b, g, t: (0, 0)),
                  pl.BlockSpec((1, HEAD_DIM_A), lambda b, g, t: (0, 0))],
        out_specs=pl.BlockSpec((tt, hw), lambda b, g, t: (b * n_t + t, g)),
        out_shape=jax.ShapeDtypeStruct((m, hk), BF16),
        scratch_shapes=[pltpu.VMEM((tt + 8, 3 * hw), F32),
                        pltpu.VMEM((tt, hw), F32),
                        pltpu.VMEM((tt, hw), F32),
                        pltpu.VMEM((tt, hw), F32),
                        pltpu.VMEM((tt, 2 * LANES), F32),
                        pltpu.VMEM((LANES, tt), F32),
                        pltpu.VMEM((HEAD_DIM_A, hw), F32)],
        compiler_params=_cparams(("parallel", "parallel", "arbitrary")),
        name="delta_mixer",
    )(proj, proj, proj, proj, proj, cw, al, dtb, norm_w.reshape(1, HEAD_DIM_A))


def _swa_kernel(sink_ref, q_ref, kc_ref, vc_ref, kp_ref, vp_ref, o_ref, bias_ref, *, steps_per_seq, qblocks):
    w = WINDOW
    grp = N_HEADS_B // N_KV_B
    hd = HEAD_DIM_B
    step = pl.program_id(0)

    @pl.when(step == 0)
    def _():
        qi = lax.broadcasted_iota(jnp.int32, (w, 2 * w), 0)
        kj = lax.broadcasted_iota(jnp.int32, (w, 2 * w), 1)
        dist = qi + w - kj
        band = (dist >= 0) & (dist < w)
        distf = dist.astype(F32)
        for head in range(N_HEADS_B):
            slope = 2.0 ** (-8.0 * (head + 1) / N_HEADS_B)
            bias_ref[0, head] = jnp.where(band, -slope * distf, -jnp.inf)
            bias_ref[1, head] = jnp.where(band & (kj >= w), -slope * distf, -jnp.inf)

    first = ((step % steps_per_seq) == 0).astype(jnp.int32)
    scale = hd ** -0.5
    ones = jnp.ones((2 * w, 3 * hd), BF16)

    def scores(n, h):
        rows = slice(n * w, (n + 1) * w)
        cols = slice(h * hd, (h + 1) * hd)
        k_prev = kp_ref[:, cols] if n == 0 else kc_ref[(n - 1) * w:n * w, cols]
        kb = jnp.concatenate([k_prev, kc_ref[rows, cols]], axis=0).astype(BF16)
        qs = jnp.concatenate([q_ref[rows, (h * grp + g) * hd:(h * grp + g + 1) * hd] for g in range(grp)], axis=0)
        return lax.dot_general((qs * scale).astype(BF16), kb, (((1,), (1,)), ((), ())),
                               preferred_element_type=F32)

    items = [(n, h) for n in range(qblocks) for h in range(N_KV_B)]
    s_next = scores(*items[0])
    for idx, (n, h) in enumerate(items):
        s = s_next
        if idx + 1 < len(items):
            s_next = scores(*items[idx + 1])
        rows = slice(n * w, (n + 1) * w)
        cols = slice(h * hd, (h + 1) * hd)
        v_prev = vp_ref[:, cols] if n == 0 else vc_ref[(n - 1) * w:n * w, cols]
        vb = jnp.concatenate([v_prev, vc_ref[rows, cols]], axis=0).astype(BF16)
        vb1 = jnp.concatenate([vb, ones], axis=1)
        es = []
        sk = []
        for g in range(grp):
            head = h * grp + g
            sink = sink_ref[head]
            bias = bias_ref[first, head] if n == 0 else bias_ref[0, head]
            logits = s[g * w:(g + 1) * w] + bias
            mx = jnp.maximum(jnp.max(logits, axis=-1, keepdims=True), sink)
            es.append(jnp.exp(logits - mx).astype(BF16))
            sk.append(jnp.exp(sink - mx))
        pv = jnp.dot(jnp.concatenate(es, axis=0), vb1, preferred_element_type=F32)
        for g in range(grp):
            head = h * grp + g
            pg = pv[g * w:(g + 1) * w]
            den = pg[:, 2 * hd:3 * hd] + sk[g]
            o_ref[rows, head * hd:(head + 1) * hd] = (pg[:, :hd] / den).astype(o_ref.dtype)


def swa_mixer(proj, sinks, *, batch, seq):
    w = WINDOW
    qblocks = 4
    tq = qblocks * w
    qd = N_HEADS_B * HEAD_DIM_B
    kd = N_KV_B * HEAD_DIM_B
    m = batch * seq
    kcol, vcol = qd // kd, qd // kd + 1
    prev = lambda i: jnp.maximum(i * qblocks - 1, 0)
    return pl.pallas_call(
        functools.partial(_swa_kernel, steps_per_seq=seq // tq, qblocks=qblocks),
        grid=(m // tq,),
        in_specs=[pl.BlockSpec(memory_space=pltpu.SMEM),
                  pl.BlockSpec((tq, qd), lambda i: (i, 0)),
                  pl.BlockSpec((tq, kd), lambda i: (i, kcol)),
                  pl.BlockSpec((tq, kd), lambda i: (i, vcol)),
                  pl.BlockSpec((w, kd), lambda i: (prev(i), kcol)),
                  pl.BlockSpec((w, kd), lambda i: (prev(i), vcol))],
        out_specs=pl.BlockSpec((tq, qd), lambda i: (i, 0)),
        out_shape=jax.ShapeDtypeStruct((m, qd), BF16),
        scratch_shapes=[pltpu.VMEM((2, N_HEADS_B, w, 2 * w), F32)],
        compiler_params=_cparams(("arbitrary",)),
        name="swa_mixer",
    )(sinks.astype(F32), proj, proj, proj, proj, proj)


def kernel(x, p, norm_mix, norm_ffn, norm_ple, norm_final, a_w_in, a_conv, a_log, a_dt_bias, a_norm, a_w_out,
           b_w_in, b_sinks, b_w_out, f_w_up, f_conv, f_w_down, ple_w_proj, ple_w_gate):
    batch, seq, d = x.shape
    depth = p.shape[0]
    m = batch * seq
    hk = N_HEADS_A * HEAD_DIM_A
    h = x.reshape(m, d)
    for i in range(depth):
        j = i // 2
        if i % 2 == 0:
            w_in = a_w_in[j]
            w_in = jnp.pad(w_in, ((0, 0), (0, 4 * hk + LANES - w_in.shape[1]))).astype(BF16)
            proj = rms_matmul(h, norm_mix[i], w_in, tm=256)
            mix = delta_mixer(proj, a_conv[j], a_log[j], a_dt_bias[j], a_norm[j], batch=batch, seq=seq)
            w_out = a_w_out[j]
        else:
            proj = rms_matmul(h, norm_mix[i], b_w_in[j].astype(BF16), tm=512)
            mix = swa_mixer(proj, b_sinks[j], batch=batch, seq=seq)
            w_out = b_w_out[j]
        h = matmul_residual(mix, w_out.astype(BF16), h, tm=512)
        h = conv_ffn(h, norm_ffn[i], f_w_up[i].astype(BF16), f_conv[i], f_w_down[i].astype(BF16),
                     tm=512, seq=seq)
        h = ple(h, p[i].reshape(m, -1), norm_ple[i], ple_w_gate[i].astype(BF16), ple_w_proj[i].astype(BF16),
                norm_final, tm=512, final=(i == depth - 1))
    return h.reshape(batch, seq, d)
```

```python
import functools
import math

import jax
import jax.numpy as jnp
from jax import lax
from jax.experimental import pallas as pl
from jax.experimental.pallas import tpu as pltpu

F32 = jnp.float32
BF16 = jnp.bfloat16

D_MODEL = 1024
N_HEADS_A = 8
HEAD_DIM_A = 128
CONV_A = 4
N_HEADS_B = 16
N_KV_B = 4
HEAD_DIM_B = 64
WINDOW = 128
D_FF = 2816
FFN_CONV = 3
PLE_DIM = 256
EPS = 1e-6

LANES = 128
BF16_ROWS = 16
VMEM_LIMIT = 56 * 1024 * 1024

LAYER_TILE = 512
DELTA_CHUNK = 128
DELTA_TILE = 1024
DELTA_HEADS = 2


def _cparams(sem):
    return pltpu.CompilerParams(dimension_semantics=sem, vmem_limit_bytes=VMEM_LIMIT)


def _rms(x, gain):
    ms = jnp.mean(x * x, axis=-1, keepdims=True)
    return x * lax.rsqrt(ms + EPS) * gain


def _sigmoid(x):
    return 0.5 + 0.5 * jnp.tanh(0.5 * x)


def _silu(x):
    h = 0.5 * x
    return h + h * jnp.tanh(h)


def _rms_matmul_kernel(x_ref, g_ref, w_ref, o_ref, *, n_chunk):
    xn = _rms(x_ref[...], g_ref[...]).astype(BF16)
    n = o_ref.shape[1]
    for c in range(0, n, n_chunk):
        w = min(n_chunk, n - c)
        o_ref[:, c:c + w] = jnp.dot(xn, w_ref[:, c:c + w], preferred_element_type=F32).astype(o_ref.dtype)


def rms_matmul(x, gain, w, *, tm, out_dtype=F32):
    m, d = x.shape
    n = w.shape[1]
    return pl.pallas_call(
        functools.partial(_rms_matmul_kernel, n_chunk=512),
        grid=(m // tm,),
        in_specs=[pl.BlockSpec((tm, d), lambda i: (i, 0)),
                  pl.BlockSpec((1, d), lambda i: (0, 0)),
                  pl.BlockSpec((d, n), lambda i: (0, 0))],
        out_specs=pl.BlockSpec((tm, n), lambda i: (i, 0)),
        out_shape=jax.ShapeDtypeStruct((m, n), out_dtype),
        compiler_params=_cparams(("parallel",)),
        name="rms_matmul",
    )(x, gain.reshape(1, d), w)


def _layer_tail_kernel(h_ref, hh_ref, mix_ref, mixh_ref, p_ref, wo_ref, gf_ref, wup_ref, cw_ref, wd_ref,
                       gp_ref, wgate_ref, wp_ref, gl_ref, o_ref, *, tiles_per_seq, final):
    hb = hh_ref.shape[0]
    mix_ext = jnp.concatenate([mixh_ref[...], mix_ref[...]], axis=0)
    h_ext = jnp.concatenate([hh_ref[...], h_ref[...]], axis=0)
    h1 = h_ext + jnp.dot(mix_ext, wo_ref[...], preferred_element_type=F32)
    first = (pl.program_id(0) % tiles_per_seq) == 0
    gain = gf_ref[...]
    xn = jnp.concatenate([jnp.where(first, 0.0, _rms(h1[:hb], gain)).astype(BF16),
                          _rms(h1[hb:], gain).astype(BF16)], axis=0)

    def conv(col):
        u = jnp.dot(xn, wup_ref[:, col:col + D_FF], preferred_element_type=F32)
        w = cw_ref[:, col:col + D_FF]
        y = w[2:3, :] * u + w[1:2, :] * pltpu.roll(u, 1, 0) + w[0:1, :] * pltpu.roll(u, 2, 0)
        return y[hb:]

    act = (_silu(conv(0)) * conv(D_FF)).astype(BF16)
    h2 = h1[hb:] + jnp.dot(act, wd_ref[...], preferred_element_type=F32)

    gate = _sigmoid(jnp.dot(_rms(h2, gp_ref[...]).astype(BF16), wgate_ref[...], preferred_element_type=F32))
    emb = jnp.dot(p_ref[...].astype(BF16), wp_ref[...], preferred_element_type=F32)
    out = h2 + gate * emb
    if final:
        out = _rms(out, gl_ref[...])
    o_ref[...] = out


def layer_tail(h, mix, p, w_out, gain_ffn, w_up, conv_w, w_down, gain_ple, w_gate, w_proj, gain_last,
               *, tm, seq, final):
    m, d = h.shape
    pd = p.shape[1]
    hb = BF16_ROWS
    row = lambda i: (i, 0)
    halo = lambda i: (jnp.maximum(i * (tm // hb) - 1, 0), 0)
    const = lambda i: (0, 0)
    resident = dict(pipeline_mode=pl.Buffered(1))
    return pl.pallas_call(
        functools.partial(_layer_tail_kernel, tiles_per_seq=seq // tm, final=final),
        grid=(m // tm,),
        in_specs=[pl.BlockSpec((tm, d), row),
                  pl.BlockSpec((hb, d), halo),
                  pl.BlockSpec((tm, d), row),
                  pl.BlockSpec((hb, d), halo),
                  pl.BlockSpec((tm, pd), row),
                  pl.BlockSpec((d, d), const, **resident),
                  pl.BlockSpec((1, d), const),
                  pl.BlockSpec((d, 2 * D_FF), const, **resident),
                  pl.BlockSpec((FFN_CONV, 2 * D_FF), const),
                  pl.BlockSpec((D_FF, d), const, **resident),
                  pl.BlockSpec((1, d), const),
                  pl.BlockSpec((d, d), const, **resident),
                  pl.BlockSpec((pd, d), const, **resident),
                  pl.BlockSpec((1, d), const)],
        out_specs=pl.BlockSpec((tm, d), row),
        out_shape=jax.ShapeDtypeStruct((m, d), F32),
        compiler_params=_cparams(("parallel",)),
        name="layer_tail",
    )(h, h, mix, mix, p, w_out, gain_ffn.reshape(1, d), w_up, conv_w, w_down,
      gain_ple.reshape(1, d), w_gate, w_proj, gain_last.reshape(1, d))


def _block_diag(x):
    xb = x.astype(BF16)
    z = jnp.zeros((x.shape[0], LANES), BF16)
    top = jnp.concatenate([xb[:, :LANES], z], axis=1)
    bot = jnp.concatenate([z, xb[:, LANES:]], axis=1)
    return jnp.concatenate([top, bot], axis=0)


def _pair_dot(a, b):
    return jnp.dot(a.astype(BF16), _block_diag(b), preferred_element_type=F32)


def _delta_kernel(q_ref, k_ref, v_ref, z_ref, ba_ref, cw_ref, al_ref, dtb_ref, nw_ref, o_ref,
                  ext_ref, qn_ref, kn_ref, vn_ref, gcol_ref, grow_ref, s_ref, *, chunk):
    tt = q_ref.shape[0]
    hw = 2 * HEAD_DIM_A
    hg = pl.program_id(1)
    tstep = pl.program_id(2)
    pad = 8
    n_chunks = tt // chunk

    @pl.when(tstep == 0)
    def _():
        ext_ref[0:pad, :] = jnp.zeros((pad, 3 * hw), F32)
        s_ref[...] = jnp.zeros(s_ref.shape, F32)

    ext_ref[pad:, 0:hw] = q_ref[...]
    ext_ref[pad:, hw:2 * hw] = k_ref[...]
    ext_ref[pad:, 2 * hw:3 * hw] = v_ref[...]
    cw = cw_ref[0]
    for part, dst in ((0, qn_ref), (1, kn_ref), (2, vn_ref)):
        lo = part * hw
        xe = ext_ref[:, lo:lo + hw]
        y = cw[part, CONV_A - 1:CONV_A, :] * xe
        for i in range(CONV_A - 1):
            y = y + cw[part, i:i + 1, :] * pltpu.roll(xe, CONV_A - 1 - i, 0)
        y = _silu(y[pad:])
        if part < 2:
            for hh in range(2):
                yh = y[:, hh * HEAD_DIM_A:(hh + 1) * HEAD_DIM_A]
                yh = yh * lax.rsqrt(jnp.sum(yh * yh, axis=-1, keepdims=True) + EPS)
                if part == 0:
                    yh = yh * (HEAD_DIM_A ** -0.5)
                dst[:, hh * HEAD_DIM_A:(hh + 1) * HEAD_DIM_A] = yh
        else:
            dst[...] = y
    ext_ref[0:pad, :] = ext_ref[tt:tt + pad, :]

    ba = ba_ref[...]
    lane = lax.broadcasted_iota(jnp.int32, ba.shape, 1)
    sp = ba + dtb_ref[...]
    softplus = jnp.maximum(sp, 0.0) + jnp.log1p(jnp.exp(-jnp.abs(sp)))
    gb = jnp.where(lane < N_HEADS_A, _sigmoid(ba), -jnp.exp(al_ref[...]) * softplus)
    gcol_ref[:, 0:LANES] = gb
    grow_ref[...] = gb.T
    g8 = grow_ref[N_HEADS_A:2 * N_HEADS_A, :]
    tl = lax.broadcasted_iota(jnp.int32, g8.shape, 1) % chunk
    sh = 1
    while sh < chunk:
        g8 = g8 + jnp.where(tl >= sh, pltpu.roll(g8, sh, 1), 0.0)
        sh *= 2
    grow_ref[N_HEADS_A:2 * N_HEADS_A, :] = g8
    gcol_ref[:, LANES:2 * LANES] = grow_ref[...].T

    h_a = hg * 2
    ii = lax.broadcasted_iota(jnp.int32, (chunk, hw), 0)
    lane2 = lax.broadcasted_iota(jnp.int32, (chunk, hw), 1)
    jj = lane2 % LANES
    first = lane2 < LANES
    first_row = lax.broadcasted_iota(jnp.int32, (1, hw), 1) < LANES
    lane_c = lax.broadcasted_iota(jnp.int32, (chunk, LANES), 1)
    row_h = lax.broadcasted_iota(jnp.int32, (N_HEADS_A, chunk), 0)
    incl = ii >= jj
    strict = ii > jj
    ixj = ii ^ jj
    eye2 = (ii == jj).astype(F32)
    nw = nw_ref[...]

    def col_pair(x, lane_a):
        a = jnp.sum(jnp.where(lane_c == lane_a, x, 0.0), axis=1, keepdims=True)
        b = jnp.sum(jnp.where(lane_c == lane_a + 1, x, 0.0), axis=1, keepdims=True)
        return jnp.where(first, a, b)

    loc = []
    for c in range(n_chunks):
        rows = slice(c * chunk, (c + 1) * chunk)
        q2, k2, v2 = qn_ref[rows, :], kn_ref[rows, :], vn_ref[rows, :]
        beta2 = col_pair(gcol_ref[rows, 0:LANES], h_a)
        gc2 = col_pair(gcol_ref[rows, LANES:2 * LANES], N_HEADS_A + h_a)
        grc = grow_ref[N_HEADS_A:2 * N_HEADS_A, rows]
        gr_a = jnp.sum(jnp.where(row_h == h_a, grc, 0.0), axis=0, keepdims=True)
        gr_b = jnp.sum(jnp.where(row_h == h_a + 1, grc, 0.0), axis=0, keepdims=True)
        gr2 = jnp.concatenate([gr_a, gr_b], axis=1)
        gl2 = jnp.where(first_row, gr_a[:, chunk - 1:chunk], gr_b[:, chunk - 1:chunk])
        decay2 = jnp.where(incl, jnp.exp(gc2 - gr2), 0.0)
        eg2 = jnp.exp(gc2)
        k2b = k2.astype(BF16)
        kq = lax.dot_general(jnp.concatenate([k2b, q2.astype(BF16)], axis=0), _block_diag(k2b),
                             (((1,), (1,)), ((), ())), preferred_element_type=F32)
        lmat = jnp.where(strict, beta2 * kq[:chunk] * decay2, 0.0)
        loc.append(dict(
            lmat=lmat, attn=(kq[chunk:] * decay2).astype(BF16),
            bv=_block_diag(beta2 * v2), bk=_block_diag((beta2 * eg2) * k2),
            qd=(q2 * eg2).astype(BF16), kd=(k2 * jnp.exp(gl2 - gc2)).astype(BF16),
            egl=jnp.exp(gl2)))

    ts = None
    b = 1
    while b < chunk:
        sel = (ixj >= b) & (ixj < 2 * b)
        xs = [jnp.where(sel, l["lmat"], 0.0) for l in loc]
        if ts is None:
            ts = [eye2 - x for x in xs]
        else:
            tbs = [t.astype(BF16) for t in ts]
            xts = [_pair_dot(x, tb) for x, tb in zip(xs, tbs)]
            ts = [t - _pair_dot(tb, xt) for t, tb, xt in zip(ts, tbs, xts)]
        b *= 2
    for l, t in zip(loc, ts):
        tb = t.astype(BF16)
        l["u0"] = jnp.dot(tb, l["bv"], preferred_element_type=F32)
        l["wq"] = jnp.concatenate(
            [jnp.dot(tb, l["bk"], preferred_element_type=F32).astype(BF16), l["qd"]], axis=0)

    s2 = s_ref[...]
    for c, l in enumerate(loc):
        rows = slice(c * chunk, (c + 1) * chunk)
        ws = jnp.dot(l["wq"], _block_diag(s2), preferred_element_type=F32)
        u_bd = _block_diag(l["u0"] - ws[:chunk])
        o2 = ws[chunk:] + jnp.dot(l["attn"], u_bd, preferred_element_type=F32)
        kd_rows = jnp.concatenate([l["kd"][:, :LANES], l["kd"][:, LANES:]], axis=0)
        s2 = l["egl"] * s2 + lax.dot_general(kd_rows, u_bd, (((0,), (0,)), ((), ())),
                                             preferred_element_type=F32)
        for hh in range(2):
            cs = slice(hh * HEAD_DIM_A, (hh + 1) * HEAD_DIM_A)
            o_ref[rows, cs] = (_rms(o2[:, cs], nw) * _silu(z_ref[rows, cs])).astype(o_ref.dtype)
    s_ref[...] = s2


def delta_mixer(proj, conv_w, a_log, dt_bias, norm_w, *, batch, seq):
    heads, chunk = DELTA_HEADS, DELTA_CHUNK
    tt = min(DELTA_TILE, seq)
    hk = N_HEADS_A * HEAD_DIM_A
    hw = heads * HEAD_DIM_A
    n_hg = N_HEADS_A // heads
    n_t = seq // tt
    m = batch * seq
    cw = conv_w.reshape(CONV_A, 3, n_hg, hw).transpose(2, 1, 0, 3)
    pad_row = jnp.zeros((1, LANES), F32)
    al = pad_row.at[0, N_HEADS_A:2 * N_HEADS_A].set(a_log.astype(F32))
    dtb = pad_row.at[0, N_HEADS_A:2 * N_HEADS_A].set(dt_bias.astype(F32))

    def col(off):
        return lambda b, g, t: (b * n_t + t, off * n_hg + g)

    return pl.pallas_call(
        functools.partial(_delta_kernel, chunk=chunk),
        grid=(batch, n_hg, n_t),
        in_specs=[pl.BlockSpec((tt, hw), col(0)),
                  pl.BlockSpec((tt, hw), col(1)),
                  pl.BlockSpec((tt, hw), col(2)),
                  pl.BlockSpec((tt, hw), col(3)),
                  pl.BlockSpec((tt, LANES), lambda b, g, t: (b * n_t + t, 4 * hk // LANES)),
                  pl.BlockSpec((1, 3, CONV_A, hw), lambda b, g, t: (g, 0, 0, 0)),
                  pl.BlockSpec((1, LANES), lambda b, g, t: (0, 0)),
                  pl.BlockSpec((1, LANES), lambda b, g, t: (0, 0)),
                  pl.BlockSpec((1, HEAD_DIM_A), lambda b, g, t: (0, 0))],
        out_specs=pl.BlockSpec((tt, hw), lambda b, g, t: (b * n_t + t, g)),
        out_shape=jax.ShapeDtypeStruct((m, hk), BF16),
        scratch_shapes=[pltpu.VMEM((tt + 8, 3 * hw), F32),
                        pltpu.VMEM((tt, hw), F32),
                        pltpu.VMEM((tt, hw), F32),
                        pltpu.VMEM((tt, hw), F32),
                        pltpu.VMEM((tt, 2 * LANES), F32),
                        pltpu.VMEM((LANES, tt), F32),
                        pltpu.VMEM((HEAD_DIM_A, hw), F32)],
        compiler_params=_cparams(("parallel", "parallel", "arbitrary")),
        name="delta_mixer",
    )(proj, proj, proj, proj, proj, cw, al, dtb, norm_w.reshape(1, HEAD_DIM_A))


def _swa_kernel(sink_ref, q_ref, kc_ref, vc_ref, kp_ref, vp_ref, o_ref, bias_ref, *, steps_per_seq, qblocks):
    w = WINDOW
    grp = N_HEADS_B // N_KV_B
    hd = HEAD_DIM_B
    step = pl.program_id(0)

    @pl.when(step == 0)
    def _():
        qi = lax.broadcasted_iota(jnp.int32, (w, 2 * w), 0)
        kj = lax.broadcasted_iota(jnp.int32, (w, 2 * w), 1)
        dist = qi + w - kj
        band = (dist >= 0) & (dist < w)
        distf = dist.astype(F32)
        for head in range(N_HEADS_B):
            slope = 2.0 ** (-8.0 * (head + 1) / N_HEADS_B)
            bias_ref[0, head] = jnp.where(band, -slope * distf, -jnp.inf)
            bias_ref[1, head] = jnp.where(band & (kj >= w), -slope * distf, -jnp.inf)

    first = ((step % steps_per_seq) == 0).astype(jnp.int32)
    scale = hd ** -0.5
    ones = jnp.ones((2 * w, 3 * hd), BF16)

    def scores(n, h):
        rows = slice(n * w, (n + 1) * w)
        cols = slice(h * hd, (h + 1) * hd)
        k_prev = kp_ref[:, cols] if n == 0 else kc_ref[(n - 1) * w:n * w, cols]
        kb = jnp.concatenate([k_prev, kc_ref[rows, cols]], axis=0).astype(BF16)
        qs = jnp.concatenate([q_ref[rows, (h * grp + g) * hd:(h * grp + g + 1) * hd] for g in range(grp)], axis=0)
        return lax.dot_general((qs * scale).astype(BF16), kb, (((1,), (1,)), ((), ())),
                               preferred_element_type=F32)

    items = [(n, h) for n in range(qblocks) for h in range(N_KV_B)]
    all_scores = [scores(n, h) for n, h in items]
    for (n, h), s in zip(items, all_scores):
        rows = slice(n * w, (n + 1) * w)
        cols = slice(h * hd, (h + 1) * hd)
        v_prev = vp_ref[:, cols] if n == 0 else vc_ref[(n - 1) * w:n * w, cols]
        vb = jnp.concatenate([v_prev, vc_ref[rows, cols]], axis=0).astype(BF16)
        vb1 = jnp.concatenate([vb, ones], axis=1)
        es = []
        sk = []
        for g in range(grp):
            head = h * grp + g
            sink = sink_ref[head]
            bias = bias_ref[first, head] if n == 0 else bias_ref[0, head]
            logits = s[g * w:(g + 1) * w] + bias
            mx = jnp.maximum(jnp.max(logits, axis=-1, keepdims=True), sink)
            es.append(jnp.exp(logits - mx).astype(BF16))
            sk.append(jnp.exp(sink - mx))
        pv = jnp.dot(jnp.concatenate(es, axis=0), vb1, preferred_element_type=F32)
        for g in range(grp):
            head = h * grp + g
            pg = pv[g * w:(g + 1) * w]
            den = pg[:, 2 * hd:3 * hd] + sk[g]
            o_ref[rows, head * hd:(head + 1) * hd] = (pg[:, :hd] / den).astype(o_ref.dtype)


def swa_mixer(proj, sinks, *, batch, seq):
    w = WINDOW
    qblocks = 4
    tq = qblocks * w
    qd = N_HEADS_B * HEAD_DIM_B
    kd = N_KV_B * HEAD_DIM_B
    m = batch * seq
    kcol, vcol = qd // kd, qd // kd + 1
    prev = lambda i: jnp.maximum(i * qblocks - 1, 0)
    return pl.pallas_call(
        functools.partial(_swa_kernel, steps_per_seq=seq // tq, qblocks=qblocks),
        grid=(m // tq,),
        in_specs=[pl.BlockSpec(memory_space=pltpu.SMEM),
                  pl.BlockSpec((tq, qd), lambda i: (i, 0)),
                  pl.BlockSpec((tq, kd), lambda i: (i, kcol)),
                  pl.BlockSpec((tq, kd), lambda i: (i, vcol)),
                  pl.BlockSpec((w, kd), lambda i: (prev(i), kcol)),
                  pl.BlockSpec((w, kd), lambda i: (prev(i), vcol))],
        out_specs=pl.BlockSpec((tq, qd), lambda i: (i, 0)),
        out_shape=jax.ShapeDtypeStruct((m, qd), BF16),
        scratch_shapes=[pltpu.VMEM((2, N_HEADS_B, w, 2 * w), F32)],
        compiler_params=_cparams(("arbitrary",)),
        name="swa_mixer",
    )(sinks.astype(F32), proj, proj, proj, proj, proj)


def kernel(x, p, norm_mix, norm_ffn, norm_ple, norm_final, a_w_in, a_conv, a_log, a_dt_bias, a_norm, a_w_out,
           b_w_in, b_sinks, b_w_out, f_w_up, f_conv, f_w_down, ple_w_proj, ple_w_gate):
    batch, seq, d = x.shape
    depth = p.shape[0]
    m = batch * seq
    hk = N_HEADS_A * HEAD_DIM_A
    h = x.reshape(m, d)
    for i in range(depth):
        j = i // 2
        if i % 2 == 0:
            w_in = a_w_in[j]
            w_in = jnp.pad(w_in, ((0, 0), (0, 4 * hk + LANES - w_in.shape[1]))).astype(BF16)
            proj = rms_matmul(h, norm_mix[i], w_in, tm=256)
            mix = delta_mixer(proj, a_conv[j], a_log[j], a_dt_bias[j], a_norm[j], batch=batch, seq=seq)
            w_out = a_w_out[j]
        else:
            proj = rms_matmul(h, norm_mix[i], b_w_in[j].astype(BF16), tm=512)
            mix = swa_mixer(proj, b_sinks[j], batch=batch, seq=seq)
            w_out = b_w_out[j]
        h = layer_tail(h, mix, p[i].reshape(m, -1), w_out.astype(BF16), norm_ffn[i], f_w_up[i].astype(BF16),
                       f_conv[i], f_w_down[i].astype(BF16), norm_ple[i], ple_w_gate[i].astype(BF16),
                       ple_w_proj[i].astype(BF16), norm_final, tm=LAYER_TILE, seq=seq, final=(i == depth - 1))
    return h.reshape(batch, seq, d)
```

```python
import functools
import math

import jax
import jax.numpy as jnp
from jax import lax
from jax.experimental import pallas as pl
from jax.experimental.pallas import tpu as pltpu

F32 = jnp.float32
BF16 = jnp.bfloat16

D_MODEL = 1024
N_HEADS_A = 8
HEAD_DIM_A = 128
CONV_A = 4
N_HEADS_B = 16
N_KV_B = 4
HEAD_DIM_B = 64
WINDOW = 128
D_FF = 2816
FFN_CONV = 3
PLE_DIM = 256
EPS = 1e-6

LANES = 128
BF16_ROWS = 16
VMEM_LIMIT = 56 * 1024 * 1024

LAYER_TILE = 512
DELTA_CHUNK = 128
DELTA_TILE = 1024
DELTA_HEADS = 2


def _cparams(sem):
    return pltpu.CompilerParams(dimension_semantics=sem, vmem_limit_bytes=VMEM_LIMIT)


def _rms(x, gain):
    ms = jnp.mean(x * x, axis=-1, keepdims=True)
    return x * lax.rsqrt(ms + EPS) * gain


def _sigmoid(x):
    return 0.5 + 0.5 * jnp.tanh(0.5 * x)


def _silu(x):
    h = 0.5 * x
    return h + h * jnp.tanh(h)


def _rms_matmul_kernel(x_ref, g_ref, w_ref, o_ref, *, n_chunk):
    xn = _rms(x_ref[...], g_ref[...]).astype(BF16)
    n = o_ref.shape[1]
    for c in range(0, n, n_chunk):
        w = min(n_chunk, n - c)
        o_ref[:, c:c + w] = jnp.dot(xn, w_ref[:, c:c + w], preferred_element_type=F32).astype(o_ref.dtype)


def rms_matmul(x, gain, w, *, tm, out_dtype=F32):
    m, d = x.shape
    n = w.shape[1]
    return pl.pallas_call(
        functools.partial(_rms_matmul_kernel, n_chunk=512),
        grid=(m // tm,),
        in_specs=[pl.BlockSpec((tm, d), lambda i: (i, 0)),
                  pl.BlockSpec((1, d), lambda i: (0, 0)),
                  pl.BlockSpec((d, n), lambda i: (0, 0))],
        out_specs=pl.BlockSpec((tm, n), lambda i: (i, 0)),
        out_shape=jax.ShapeDtypeStruct((m, n), out_dtype),
        compiler_params=_cparams(("parallel",)),
        name="rms_matmul",
    )(x, gain.reshape(1, d), w)


def _delta_in_kernel(x_ref, xh_ref, g_ref, w_ref, cw_ref, o_ref, *, tiles_per_seq, n_chunk):
    hb = xh_ref.shape[0]
    hk = N_HEADS_A * HEAD_DIM_A
    gain = g_ref[...]
    first = (pl.program_id(0) % tiles_per_seq) == 0
    xm = _rms(x_ref[...], gain).astype(BF16)
    xe = jnp.concatenate([jnp.where(first, 0.0, _rms(xh_ref[...], gain)).astype(BF16), xm], axis=0)
    for c in range(0, 3 * hk, n_chunk):
        u = jnp.dot(xe, w_ref[:, c:c + n_chunk], preferred_element_type=F32)
        w = cw_ref[:, c:c + n_chunk]
        y = w[CONV_A - 1:CONV_A, :] * u
        for i in range(CONV_A - 1):
            y = y + w[i:i + 1, :] * pltpu.roll(u, CONV_A - 1 - i, 0)
        y = _silu(y[hb:])
        if c < 2 * hk:
            for j in range(0, n_chunk, HEAD_DIM_A):
                yh = y[:, j:j + HEAD_DIM_A]
                yh = yh * lax.rsqrt(jnp.sum(yh * yh, axis=-1, keepdims=True) + EPS)
                if c < hk:
                    yh = yh * (HEAD_DIM_A ** -0.5)
                o_ref[:, c + j:c + j + HEAD_DIM_A] = yh
        else:
            o_ref[:, c:c + n_chunk] = y
    n = o_ref.shape[1]
    for c in range(3 * hk, n, n_chunk):
        wd = min(n_chunk, n - c)
        o_ref[:, c:c + wd] = jnp.dot(xm, w_ref[:, c:c + wd], preferred_element_type=F32)


def delta_in_proj(x, gain, w, conv_w, *, tm, seq):
    m, d = x.shape
    n = w.shape[1]
    hb = BF16_ROWS
    return pl.pallas_call(
        functools.partial(_delta_in_kernel, tiles_per_seq=seq // tm, n_chunk=512),
        grid=(m // tm,),
        in_specs=[pl.BlockSpec((tm, d), lambda i: (i, 0)),
                  pl.BlockSpec((hb, d), lambda i: (jnp.maximum(i * (tm // hb) - 1, 0), 0)),
                  pl.BlockSpec((1, d), lambda i: (0, 0)),
                  pl.BlockSpec((d, n), lambda i: (0, 0), pipeline_mode=pl.Buffered(1)),
                  pl.BlockSpec(conv_w.shape, lambda i: (0, 0))],
        out_specs=pl.BlockSpec((tm, n), lambda i: (i, 0)),
        out_shape=jax.ShapeDtypeStruct((m, n), F32),
        compiler_params=_cparams(("parallel",)),
        name="delta_in_proj",
    )(x, x, gain.reshape(1, d), w, conv_w)


def _layer_tail_kernel(h_ref, hh_ref, mix_ref, mixh_ref, p_ref, wo_ref, gf_ref, wup_ref, cw_ref, wd_ref,
                       gp_ref, wgate_ref, wp_ref, gl_ref, o_ref, *, tiles_per_seq, final):
    hb = hh_ref.shape[0]
    mix_ext = jnp.concatenate([mixh_ref[...], mix_ref[...]], axis=0)
    h_ext = jnp.concatenate([hh_ref[...], h_ref[...]], axis=0)
    h1 = h_ext + jnp.dot(mix_ext, wo_ref[...], preferred_element_type=F32)
    first = (pl.program_id(0) % tiles_per_seq) == 0
    gain = gf_ref[...]
    xn = jnp.concatenate([jnp.where(first, 0.0, _rms(h1[:hb], gain)).astype(BF16),
                          _rms(h1[hb:], gain).astype(BF16)], axis=0)

    def conv(col):
        u = jnp.dot(xn, wup_ref[:, col:col + D_FF], preferred_element_type=F32)
        w = cw_ref[:, col:col + D_FF]
        y = w[2:3, :] * u + w[1:2, :] * pltpu.roll(u, 1, 0) + w[0:1, :] * pltpu.roll(u, 2, 0)
        return y[hb:]

    act = (_silu(conv(0)) * conv(D_FF)).astype(BF16)
    h2 = h1[hb:] + jnp.dot(act, wd_ref[...], preferred_element_type=F32)

    gate = _sigmoid(jnp.dot(_rms(h2, gp_ref[...]).astype(BF16), wgate_ref[...], preferred_element_type=F32))
    emb = jnp.dot(p_ref[...].astype(BF16), wp_ref[...], preferred_element_type=F32)
    out = h2 + gate * emb
    if final:
        out = _rms(out, gl_ref[...])
    o_ref[...] = out


def layer_tail(h, mix, p, w_out, gain_ffn, w_up, conv_w, w_down, gain_ple, w_gate, w_proj, gain_last,
               *, tm, seq, final):
    m, d = h.shape
    pd = p.shape[1]
    hb = BF16_ROWS
    row = lambda i: (i, 0)
    halo = lambda i: (jnp.maximum(i * (tm // hb) - 1, 0), 0)
    const = lambda i: (0, 0)
    resident = dict(pipeline_mode=pl.Buffered(1))
    return pl.pallas_call(
        functools.partial(_layer_tail_kernel, tiles_per_seq=seq // tm, final=final),
        grid=(m // tm,),
        in_specs=[pl.BlockSpec((tm, d), row),
                  pl.BlockSpec((hb, d), halo),
                  pl.BlockSpec((tm, d), row),
                  pl.BlockSpec((hb, d), halo),
                  pl.BlockSpec((tm, pd), row),
                  pl.BlockSpec((d, d), const, **resident),
                  pl.BlockSpec((1, d), const),
                  pl.BlockSpec((d, 2 * D_FF), const, **resident),
                  pl.BlockSpec((FFN_CONV, 2 * D_FF), const),
                  pl.BlockSpec((D_FF, d), const, **resident),
                  pl.BlockSpec((1, d), const),
                  pl.BlockSpec((d, d), const, **resident),
                  pl.BlockSpec((pd, d), const, **resident),
                  pl.BlockSpec((1, d), const)],
        out_specs=pl.BlockSpec((tm, d), row),
        out_shape=jax.ShapeDtypeStruct((m, d), F32),
        compiler_params=_cparams(("parallel",)),
        name="layer_tail",
    )(h, h, mix, mix, p, w_out, gain_ffn.reshape(1, d), w_up, conv_w, w_down,
      gain_ple.reshape(1, d), w_gate, w_proj, gain_last.reshape(1, d))


def _block_diag(x):
    xb = x.astype(BF16)
    z = jnp.zeros((x.shape[0], LANES), BF16)
    top = jnp.concatenate([xb[:, :LANES], z], axis=1)
    bot = jnp.concatenate([z, xb[:, LANES:]], axis=1)
    return jnp.concatenate([top, bot], axis=0)


def _pair_dot(a, b):
    return jnp.dot(a.astype(BF16), _block_diag(b), preferred_element_type=F32)


def _delta_kernel(q_ref, k_ref, v_ref, z_ref, ba_ref, al_ref, dtb_ref, nw_ref, o_ref,
                  gcol_ref, grow_ref, pq_ref, n_ref, oc_ref, egl_ref, s_ref, *, chunk, n_t, n_hg):
    tt = q_ref.shape[0]
    hw = 2 * HEAD_DIM_A
    step = pl.program_id(0)
    n_tiles = pl.num_programs(0) - 1
    n_chunks = tt // chunk
    slot = step % 2
    prev = 1 - slot
    tile = jnp.minimum(step, n_tiles - 1)
    h_a = ((tile // n_t) % n_hg) * 2

    @pl.when(step == 0)
    def _():
        pq_ref[1] = jnp.zeros(pq_ref.shape[1:], BF16)
        n_ref[1] = jnp.zeros(n_ref.shape[1:], F32)
        oc_ref[1] = jnp.zeros(oc_ref.shape[1:], F32)
        egl_ref[1] = jnp.zeros(egl_ref.shape[1:], F32)
        s_ref[...] = jnp.zeros(s_ref.shape, F32)

    ba = ba_ref[...]
    lane = lax.broadcasted_iota(jnp.int32, ba.shape, 1)
    sp = ba + dtb_ref[...]
    softplus = jnp.maximum(sp, 0.0) + jnp.log1p(jnp.exp(-jnp.abs(sp)))
    gb = jnp.where(lane < N_HEADS_A, _sigmoid(ba), -jnp.exp(al_ref[...]) * softplus)
    gcol_ref[:, 0:LANES] = gb
    grow_ref[...] = gb.T
    g8 = grow_ref[N_HEADS_A:2 * N_HEADS_A, :]
    tl = lax.broadcasted_iota(jnp.int32, g8.shape, 1) % chunk
    sh = 1
    while sh < chunk:
        g8 = g8 + jnp.where(tl >= sh, pltpu.roll(g8, sh, 1), 0.0)
        sh *= 2
    grow_ref[N_HEADS_A:2 * N_HEADS_A, :] = g8
    gcol_ref[:, LANES:2 * LANES] = grow_ref[...].T

    ii = lax.broadcasted_iota(jnp.int32, (chunk, hw), 0)
    lane2 = lax.broadcasted_iota(jnp.int32, (chunk, hw), 1)
    jj = lane2 % LANES
    first = lane2 < LANES
    first_row = lax.broadcasted_iota(jnp.int32, (1, hw), 1) < LANES
    lane_c = lax.broadcasted_iota(jnp.int32, (chunk, LANES), 1)
    row_h = lax.broadcasted_iota(jnp.int32, (N_HEADS_A, chunk), 0)
    incl = ii >= jj
    strict = ii > jj
    ixj = ii ^ jj
    eye2 = (ii == jj).astype(F32)
    nw = nw_ref[...]

    def col_pair(x, lane_a):
        a = jnp.sum(jnp.where(lane_c == lane_a, x, 0.0), axis=1, keepdims=True)
        b = jnp.sum(jnp.where(lane_c == lane_a + 1, x, 0.0), axis=1, keepdims=True)
        return jnp.where(first, a, b)

    def prep(c):
        rows = slice(c * chunk, (c + 1) * chunk)
        q2, k2, v2 = q_ref[rows, :], k_ref[rows, :], v_ref[rows, :]
        beta2 = col_pair(gcol_ref[rows, 0:LANES], h_a)
        gc2 = col_pair(gcol_ref[rows, LANES:2 * LANES], N_HEADS_A + h_a)
        grc = grow_ref[N_HEADS_A:2 * N_HEADS_A, rows]
        gr_a = jnp.sum(jnp.where(row_h == h_a, grc, 0.0), axis=0, keepdims=True)
        gr_b = jnp.sum(jnp.where(row_h == h_a + 1, grc, 0.0), axis=0, keepdims=True)
        gr2 = jnp.concatenate([gr_a, gr_b], axis=1)
        gl2 = jnp.where(first_row, gr_a[:, chunk - 1:chunk], gr_b[:, chunk - 1:chunk])
        decay2 = jnp.where(incl, jnp.exp(gc2 - gr2), 0.0)
        eg2 = jnp.exp(gc2)
        k2b = k2.astype(BF16)
        kq = lax.dot_general(jnp.concatenate([k2b, q2.astype(BF16)], axis=0), _block_diag(k2b),
                             (((1,), (1,)), ((), ())), preferred_element_type=F32)
        kd2 = k2 * jnp.exp(gl2 - gc2)
        kdt = jnp.concatenate([kd2[:, :LANES].T, kd2[:, LANES:].T], axis=1)
        egl_ref[slot, c] = jnp.exp(gl2)
        return dict(
            c=c, lmat=jnp.where(strict, beta2 * kq[:chunk] * decay2, 0.0),
            ka=jnp.concatenate([kdt, kq[chunk:] * decay2], axis=0).astype(BF16),
            bv=_block_diag(beta2 * v2), bk=_block_diag((beta2 * eg2) * k2), qd=q2 * eg2)

    def local_stages(group):
        ts = None
        b = 1
        while b < chunk:
            sel = (ixj >= b) & (ixj < 2 * b)
            xs = [jnp.where(sel, l["lmat"], 0.0) for l in group]
            if ts is None:
                ts = [eye2 - x for x in xs]
            else:
                tbs = [t.astype(BF16) for t in ts]
                xts = [_pair_dot(x, tb) for x, tb in zip(xs, tbs)]
                ts = [t - _pair_dot(tb, xt) for t, tb, xt in zip(ts, tbs, xts)]
                yield
            b *= 2
        sols = []
        for l, t in zip(group, ts):
            tb = t.astype(BF16)
            sols.append((jnp.dot(tb, l["bv"], preferred_element_type=F32),
                         jnp.dot(tb, l["bk"], preferred_element_type=F32)))
        yield
        for l, (u0, wk) in zip(group, sols):
            pw = jnp.dot(l["ka"], _block_diag(wk), preferred_element_type=F32)
            nu = jnp.dot(l["ka"], _block_diag(u0), preferred_element_type=F32)
            pq_ref[slot, l["c"]] = jnp.concatenate(
                [pw[:HEAD_DIM_A], l["qd"] - pw[HEAD_DIM_A:]], axis=0).astype(BF16)
            n_ref[slot, l["c"]] = nu[:HEAD_DIM_A]
            oc_ref[slot, l["c"]] = nu[HEAD_DIM_A:]
        yield

    def state_step(c, s2):
        res = jnp.dot(pq_ref[prev, c], _block_diag(s2), preferred_element_type=F32)
        o2 = res[HEAD_DIM_A:] + oc_ref[prev, c]
        rows = slice(c * chunk, (c + 1) * chunk)
        for hh in range(2):
            cs = slice(hh * HEAD_DIM_A, (hh + 1) * HEAD_DIM_A)
            o_ref[rows, cs] = (_rms(o2[:, cs], nw) * _silu(z_ref[rows, cs])).astype(o_ref.dtype)
        return egl_ref[prev, c] * s2 + n_ref[prev, c] - res[:HEAD_DIM_A]

    fresh = ((step - 1) % n_t) == 0
    s2 = jnp.where(fresh, 0.0, s_ref[...])
    group = [prep(c) for c in range(n_chunks)]
    done = 0
    for _ in local_stages(group):
        if done < n_chunks:
            s2 = state_step(done, s2)
            done += 1
    while done < n_chunks:
        s2 = state_step(done, s2)
        done += 1
    s_ref[...] = s2


def delta_mixer(proj, a_log, dt_bias, norm_w, *, batch, seq):
    heads, chunk = DELTA_HEADS, DELTA_CHUNK
    tt = min(DELTA_TILE, seq)
    hk = N_HEADS_A * HEAD_DIM_A
    hw = heads * HEAD_DIM_A
    n_hg = N_HEADS_A // heads
    n_t = seq // tt
    n_c = tt // chunk
    n_tiles = batch * n_hg * n_t
    m = batch * seq
    pad_row = jnp.zeros((1, LANES), F32)
    al = pad_row.at[0, N_HEADS_A:2 * N_HEADS_A].set(a_log.astype(F32))
    dtb = pad_row.at[0, N_HEADS_A:2 * N_HEADS_A].set(dt_bias.astype(F32))

    def block(off, lag):
        def index(s):
            tile = jnp.clip(s - lag, 0, n_tiles - 1)
            row, t = tile // n_t, tile % n_t
            return ((row // n_hg) * n_t + t, off * n_hg + row % n_hg)
        return index

    def gate_block(s):
        tile = jnp.minimum(s, n_tiles - 1)
        return (((tile // n_t) // n_hg) * n_t + tile % n_t, 4 * hk // LANES)

    const = lambda s: (0, 0)
    return pl.pallas_call(
        functools.partial(_delta_kernel, chunk=chunk, n_t=n_t, n_hg=n_hg),
        grid=(n_tiles + 1,),
        in_specs=[pl.BlockSpec((tt, hw), block(0, 0)),
                  pl.BlockSpec((tt, hw), block(1, 0)),
                  pl.BlockSpec((tt, hw), block(2, 0)),
                  pl.BlockSpec((tt, hw), block(3, 1)),
                  pl.BlockSpec((tt, LANES), gate_block),
                  pl.BlockSpec((1, LANES), const),
                  pl.BlockSpec((1, LANES), const),
                  pl.BlockSpec((1, HEAD_DIM_A), const)],
        out_specs=pl.BlockSpec((tt, hw), block(0, 1)),
        out_shape=jax.ShapeDtypeStruct((m, hk), BF16),
        scratch_shapes=[pltpu.VMEM((tt, 2 * LANES), F32),
                        pltpu.VMEM((LANES, tt), F32),
                        pltpu.VMEM((2, n_c, HEAD_DIM_A + chunk, hw), BF16),
                        pltpu.VMEM((2, n_c, HEAD_DIM_A, hw), F32),
                        pltpu.VMEM((2, n_c, chunk, hw), F32),
                        pltpu.VMEM((2, n_c, 1, hw), F32),
                        pltpu.VMEM((HEAD_DIM_A, hw), F32)],
        compiler_params=_cparams(("arbitrary",)),
        name="delta_mixer",
    )(proj, proj, proj, proj, proj, al, dtb, norm_w.reshape(1, HEAD_DIM_A))


def _swa_kernel(sink_ref, q_ref, kc_ref, vc_ref, kp_ref, vp_ref, o_ref, bias_ref, *, steps_per_seq, qblocks):
    w = WINDOW
    grp = N_HEADS_B // N_KV_B
    hd = HEAD_DIM_B
    step = pl.program_id(0)

    @pl.when(step == 0)
    def _():
        qi = lax.broadcasted_iota(jnp.int32, (w, 2 * w), 0)
        kj = lax.broadcasted_iota(jnp.int32, (w, 2 * w), 1)
        dist = qi + w - kj
        band = (dist >= 0) & (dist < w)
        distf = dist.astype(F32)
        for head in range(N_HEADS_B):
            slope = 2.0 ** (-8.0 * (head + 1) / N_HEADS_B)
            bias_ref[0, head] = jnp.where(band, -slope * distf, -jnp.inf)
            bias_ref[1, head] = jnp.where(band & (kj >= w), -slope * distf, -jnp.inf)

    first = ((step % steps_per_seq) == 0).astype(jnp.int32)
    scale = hd ** -0.5
    ones = jnp.ones((2 * w, 3 * hd), BF16)

    def scores(n, h):
        rows = slice(n * w, (n + 1) * w)
        cols = slice(h * hd, (h + 1) * hd)
        k_prev = kp_ref[:, cols] if n == 0 else kc_ref[(n - 1) * w:n * w, cols]
        kb = jnp.concatenate([k_prev, kc_ref[rows, cols]], axis=0).astype(BF16)
        qs = jnp.concatenate([q_ref[rows, (h * grp + g) * hd:(h * grp + g + 1) * hd] for g in range(grp)], axis=0)
        return lax.dot_general((qs * scale).astype(BF16), kb, (((1,), (1,)), ((), ())),
                               preferred_element_type=F32)

    items = [(n, h) for n in range(qblocks) for h in range(N_KV_B)]
    all_scores = [scores(n, h) for n, h in items]
    for (n, h), s in zip(items, all_scores):
        rows = slice(n * w, (n + 1) * w)
        cols = slice(h * hd, (h + 1) * hd)
        v_prev = vp_ref[:, cols] if n == 0 else vc_ref[(n - 1) * w:n * w, cols]
        vb = jnp.concatenate([v_prev, vc_ref[rows, cols]], axis=0).astype(BF16)
        vb1 = jnp.concatenate([vb, ones], axis=1)
        es = []
        sk = []
        for g in range(grp):
            head = h * grp + g
            sink = sink_ref[head]
            bias = bias_ref[first, head] if n == 0 else bias_ref[0, head]
            logits = s[g * w:(g + 1) * w] + bias
            mx = jnp.maximum(jnp.max(logits, axis=-1, keepdims=True), sink)
            es.append(jnp.exp(logits - mx).astype(BF16))
            sk.append(jnp.exp(sink - mx))
        pv = jnp.dot(jnp.concatenate(es, axis=0), vb1, preferred_element_type=F32)
        for g in range(grp):
            head = h * grp + g
            pg = pv[g * w:(g + 1) * w]
            den = pg[:, 2 * hd:3 * hd] + sk[g]
            o_ref[rows, head * hd:(head + 1) * hd] = (pg[:, :hd] / den).astype(o_ref.dtype)


def swa_mixer(proj, sinks, *, batch, seq):
    w = WINDOW
    qblocks = 4
    tq = qblocks * w
    qd = N_HEADS_B * HEAD_DIM_B
    kd = N_KV_B * HEAD_DIM_B
    m = batch * seq
    kcol, vcol = qd // kd, qd // kd + 1
    prev = lambda i: jnp.maximum(i * qblocks - 1, 0)
    return pl.pallas_call(
        functools.partial(_swa_kernel, steps_per_seq=seq // tq, qblocks=qblocks),
        grid=(m // tq,),
        in_specs=[pl.BlockSpec(memory_space=pltpu.SMEM),
                  pl.BlockSpec((tq, qd), lambda i: (i, 0)),
                  pl.BlockSpec((tq, kd), lambda i: (i, kcol)),
                  pl.BlockSpec((tq, kd), lambda i: (i, vcol)),
                  pl.BlockSpec((w, kd), lambda i: (prev(i), kcol)),
                  pl.BlockSpec((w, kd), lambda i: (prev(i), vcol))],
        out_specs=pl.BlockSpec((tq, qd), lambda i: (i, 0)),
        out_shape=jax.ShapeDtypeStruct((m, qd), BF16),
        scratch_shapes=[pltpu.VMEM((2, N_HEADS_B, w, 2 * w), F32)],
        compiler_params=_cparams(("arbitrary",)),
        name="swa_mixer",
    )(sinks.astype(F32), proj, proj, proj, proj, proj)


def kernel(x, p, norm_mix, norm_ffn, norm_ple, norm_final, a_w_in, a_conv, a_log, a_dt_bias, a_norm, a_w_out,
           b_w_in, b_sinks, b_w_out, f_w_up, f_conv, f_w_down, ple_w_proj, ple_w_gate):
    batch, seq, d = x.shape
    depth = p.shape[0]
    m = batch * seq
    hk = N_HEADS_A * HEAD_DIM_A
    h = x.reshape(m, d)
    for i in range(depth):
        j = i // 2
        if i % 2 == 0:
            w_in = a_w_in[j]
            w_in = jnp.pad(w_in, ((0, 0), (0, 4 * hk + LANES - w_in.shape[1]))).astype(BF16)
            conv_w = jnp.pad(a_conv[j], ((0, 0), (0, w_in.shape[1] - a_conv[j].shape[1])))
            proj = delta_in_proj(h, norm_mix[i], w_in, conv_w, tm=LAYER_TILE, seq=seq)
            mix = delta_mixer(proj, a_log[j], a_dt_bias[j], a_norm[j], batch=batch, seq=seq)
            w_out = a_w_out[j]
        else:
            proj = rms_matmul(h, norm_mix[i], b_w_in[j].astype(BF16), tm=512)
            mix = swa_mixer(proj, b_sinks[j], batch=batch, seq=seq)
            w_out = b_w_out[j]
        h = layer_tail(h, mix, p[i].reshape(m, -1), w_out.astype(BF16), norm_ffn[i], f_w_up[i].astype(BF16),
                       f_conv[i], f_w_down[i].astype(BF16), norm_ple[i], ple_w_gate[i].astype(BF16),
                       ple_w_proj[i].astype(BF16), norm_final, tm=LAYER_TILE, seq=seq, final=(i == depth - 1))
    return h.reshape(batch, seq, d)
```

```python
import functools
import math

import jax
import jax.numpy as jnp
from jax import lax
from jax.experimental import pallas as pl
from jax.experimental.pallas import tpu as pltpu

F32 = jnp.float32
BF16 = jnp.bfloat16

D_MODEL = 1024
N_HEADS_A = 8
HEAD_DIM_A = 128
CONV_A = 4
N_HEADS_B = 16
N_KV_B = 4
HEAD_DIM_B = 64
WINDOW = 128
D_FF = 2816
FFN_CONV = 3
PLE_DIM = 256
EPS = 1e-6

LANES = 128
BF16_ROWS = 16
VMEM_LIMIT = 56 * 1024 * 1024

LAYER_TILE = 512
DELTA_CHUNK = 128
DELTA_TILE = 1024
DELTA_HEADS = 4


def _cparams(sem):
    return pltpu.CompilerParams(dimension_semantics=sem, vmem_limit_bytes=VMEM_LIMIT)


def _rms(x, gain):
    ms = jnp.mean(x * x, axis=-1, keepdims=True)
    return x * lax.rsqrt(ms + EPS) * gain


def _sigmoid(x):
    return 0.5 + 0.5 * jnp.tanh(0.5 * x)


def _silu(x):
    h = 0.5 * x
    return h + h * jnp.tanh(h)


def _rms_matmul_kernel(x_ref, g_ref, w_ref, o_ref, *, n_chunk):
    xn = _rms(x_ref[...], g_ref[...]).astype(BF16)
    n = o_ref.shape[1]
    for c in range(0, n, n_chunk):
        w = min(n_chunk, n - c)
        o_ref[:, c:c + w] = jnp.dot(xn, w_ref[:, c:c + w], preferred_element_type=F32).astype(o_ref.dtype)


def rms_matmul(x, gain, w, *, tm, out_dtype=F32):
    m, d = x.shape
    n = w.shape[1]
    return pl.pallas_call(
        functools.partial(_rms_matmul_kernel, n_chunk=512),
        grid=(m // tm,),
        in_specs=[pl.BlockSpec((tm, d), lambda i: (i, 0)),
                  pl.BlockSpec((1, d), lambda i: (0, 0)),
                  pl.BlockSpec((d, n), lambda i: (0, 0))],
        out_specs=pl.BlockSpec((tm, n), lambda i: (i, 0)),
        out_shape=jax.ShapeDtypeStruct((m, n), out_dtype),
        compiler_params=_cparams(("parallel",)),
        name="rms_matmul",
    )(x, gain.reshape(1, d), w)


def _delta_in_kernel(x_ref, xh_ref, g_ref, w_ref, cw_ref, o_ref, *, tiles_per_seq, n_chunk):
    hb = xh_ref.shape[0]
    hk = N_HEADS_A * HEAD_DIM_A
    gain = g_ref[...]
    first = (pl.program_id(0) % tiles_per_seq) == 0
    xm = _rms(x_ref[...], gain).astype(BF16)
    xe = jnp.concatenate([jnp.where(first, 0.0, _rms(xh_ref[...], gain)).astype(BF16), xm], axis=0)
    for c in range(0, 3 * hk, n_chunk):
        u = jnp.dot(xe, w_ref[:, c:c + n_chunk], preferred_element_type=F32)
        w = cw_ref[:, c:c + n_chunk]
        y = w[CONV_A - 1:CONV_A, :] * u
        for i in range(CONV_A - 1):
            y = y + w[i:i + 1, :] * pltpu.roll(u, CONV_A - 1 - i, 0)
        y = _silu(y[hb:])
        if c < 2 * hk:
            for j in range(0, n_chunk, HEAD_DIM_A):
                yh = y[:, j:j + HEAD_DIM_A]
                yh = yh * lax.rsqrt(jnp.sum(yh * yh, axis=-1, keepdims=True) + EPS)
                if c < hk:
                    yh = yh * (HEAD_DIM_A ** -0.5)
                o_ref[:, c + j:c + j + HEAD_DIM_A] = yh
        else:
            o_ref[:, c:c + n_chunk] = y
    n = o_ref.shape[1]
    for c in range(3 * hk, n, n_chunk):
        wd = min(n_chunk, n - c)
        o_ref[:, c:c + wd] = jnp.dot(xm, w_ref[:, c:c + wd], preferred_element_type=F32)


def delta_in_proj(x, gain, w, conv_w, *, tm, seq):
    m, d = x.shape
    n = w.shape[1]
    hb = BF16_ROWS
    return pl.pallas_call(
        functools.partial(_delta_in_kernel, tiles_per_seq=seq // tm, n_chunk=512),
        grid=(m // tm,),
        in_specs=[pl.BlockSpec((tm, d), lambda i: (i, 0)),
                  pl.BlockSpec((hb, d), lambda i: (jnp.maximum(i * (tm // hb) - 1, 0), 0)),
                  pl.BlockSpec((1, d), lambda i: (0, 0)),
                  pl.BlockSpec((d, n), lambda i: (0, 0), pipeline_mode=pl.Buffered(1)),
                  pl.BlockSpec(conv_w.shape, lambda i: (0, 0))],
        out_specs=pl.BlockSpec((tm, n), lambda i: (i, 0)),
        out_shape=jax.ShapeDtypeStruct((m, n), F32),
        compiler_params=_cparams(("parallel",)),
        name="delta_in_proj",
    )(x, x, gain.reshape(1, d), w, conv_w)


def _layer_tail_kernel(h_ref, hh_ref, mix_ref, mixh_ref, p_ref, wo_ref, gf_ref, wup_ref, cw_ref, wd_ref,
                       gp_ref, wgate_ref, wp_ref, gl_ref, o_ref, *, tiles_per_seq, final):
    hb = hh_ref.shape[0]
    mix_ext = jnp.concatenate([mixh_ref[...], mix_ref[...]], axis=0)
    h_ext = jnp.concatenate([hh_ref[...], h_ref[...]], axis=0)
    h1 = h_ext + jnp.dot(mix_ext, wo_ref[...], preferred_element_type=F32)
    first = (pl.program_id(0) % tiles_per_seq) == 0
    gain = gf_ref[...]
    xn = jnp.concatenate([jnp.where(first, 0.0, _rms(h1[:hb], gain)).astype(BF16),
                          _rms(h1[hb:], gain).astype(BF16)], axis=0)

    def conv(col):
        u = jnp.dot(xn, wup_ref[:, col:col + D_FF], preferred_element_type=F32)
        w = cw_ref[:, col:col + D_FF]
        y = w[2:3, :] * u + w[1:2, :] * pltpu.roll(u, 1, 0) + w[0:1, :] * pltpu.roll(u, 2, 0)
        return y[hb:]

    act = (_silu(conv(0)) * conv(D_FF)).astype(BF16)
    h2 = h1[hb:] + jnp.dot(act, wd_ref[...], preferred_element_type=F32)

    gate = _sigmoid(jnp.dot(_rms(h2, gp_ref[...]).astype(BF16), wgate_ref[...], preferred_element_type=F32))
    emb = jnp.dot(p_ref[...].astype(BF16), wp_ref[...], preferred_element_type=F32)
    out = h2 + gate * emb
    if final:
        out = _rms(out, gl_ref[...])
    o_ref[...] = out


def layer_tail(h, mix, p, w_out, gain_ffn, w_up, conv_w, w_down, gain_ple, w_gate, w_proj, gain_last,
               *, tm, seq, final):
    m, d = h.shape
    pd = p.shape[1]
    hb = BF16_ROWS
    row = lambda i: (i, 0)
    halo = lambda i: (jnp.maximum(i * (tm // hb) - 1, 0), 0)
    const = lambda i: (0, 0)
    resident = dict(pipeline_mode=pl.Buffered(1))
    return pl.pallas_call(
        functools.partial(_layer_tail_kernel, tiles_per_seq=seq // tm, final=final),
        grid=(m // tm,),
        in_specs=[pl.BlockSpec((tm, d), row),
                  pl.BlockSpec((hb, d), halo),
                  pl.BlockSpec((tm, d), row),
                  pl.BlockSpec((hb, d), halo),
                  pl.BlockSpec((tm, pd), row),
                  pl.BlockSpec((d, d), const, **resident),
                  pl.BlockSpec((1, d), const),
                  pl.BlockSpec((d, 2 * D_FF), const, **resident),
                  pl.BlockSpec((FFN_CONV, 2 * D_FF), const),
                  pl.BlockSpec((D_FF, d), const, **resident),
                  pl.BlockSpec((1, d), const),
                  pl.BlockSpec((d, d), const, **resident),
                  pl.BlockSpec((pd, d), const, **resident),
                  pl.BlockSpec((1, d), const)],
        out_specs=pl.BlockSpec((tm, d), row),
        out_shape=jax.ShapeDtypeStruct((m, d), F32),
        compiler_params=_cparams(("parallel",)),
        name="layer_tail",
    )(h, h, mix, mix, p, w_out, gain_ffn.reshape(1, d), w_up, conv_w, w_down,
      gain_ple.reshape(1, d), w_gate, w_proj, gain_last.reshape(1, d))


def _block_diag(x):
    xb = x.astype(BF16)
    z = jnp.zeros((x.shape[0], LANES), BF16)
    top = jnp.concatenate([xb[:, :LANES], z], axis=1)
    bot = jnp.concatenate([z, xb[:, LANES:]], axis=1)
    return jnp.concatenate([top, bot], axis=0)


def _pair_dot(a, b):
    return jnp.dot(a.astype(BF16), _block_diag(b), preferred_element_type=F32)


def _delta_kernel(q_ref, k_ref, v_ref, z_ref, ba_ref, al_ref, dtb_ref, nw_ref, o_ref,
                  gcol_ref, grow_ref, pq_ref, n_ref, oc_ref, egl_ref, s_ref, *, chunk, n_t, n_hg):
    tt = q_ref.shape[0]
    hw = 2 * HEAD_DIM_A
    n_pairs = q_ref.shape[1] // hw
    step = pl.program_id(0)
    n_tiles = pl.num_programs(0) - 1
    n_chunks = tt // chunk
    slot = step % 2
    prev = 1 - slot
    tile = jnp.minimum(step, n_tiles - 1)
    head0 = ((tile // n_t) % n_hg) * (2 * n_pairs)

    @pl.when(step == 0)
    def _():
        pq_ref[1] = jnp.zeros(pq_ref.shape[1:], BF16)
        n_ref[1] = jnp.zeros(n_ref.shape[1:], F32)
        oc_ref[1] = jnp.zeros(oc_ref.shape[1:], F32)
        egl_ref[1] = jnp.zeros(egl_ref.shape[1:], F32)
        s_ref[...] = jnp.zeros(s_ref.shape, F32)

    ba = ba_ref[...]
    lane = lax.broadcasted_iota(jnp.int32, ba.shape, 1)
    sp = ba + dtb_ref[...]
    softplus = jnp.maximum(sp, 0.0) + jnp.log1p(jnp.exp(-jnp.abs(sp)))
    gb = jnp.where(lane < N_HEADS_A, _sigmoid(ba), -jnp.exp(al_ref[...]) * softplus)
    gcol_ref[:, 0:LANES] = gb
    grow_ref[...] = gb.T
    g8 = grow_ref[N_HEADS_A:2 * N_HEADS_A, :]
    tl = lax.broadcasted_iota(jnp.int32, g8.shape, 1) % chunk
    sh = 1
    while sh < chunk:
        g8 = g8 + jnp.where(tl >= sh, pltpu.roll(g8, sh, 1), 0.0)
        sh *= 2
    grow_ref[N_HEADS_A:2 * N_HEADS_A, :] = g8
    gcol_ref[:, LANES:2 * LANES] = grow_ref[...].T

    ii = lax.broadcasted_iota(jnp.int32, (chunk, hw), 0)
    lane2 = lax.broadcasted_iota(jnp.int32, (chunk, hw), 1)
    jj = lane2 % LANES
    first = lane2 < LANES
    first_row = lax.broadcasted_iota(jnp.int32, (1, hw), 1) < LANES
    lane_c = lax.broadcasted_iota(jnp.int32, (chunk, LANES), 1)
    row_h = lax.broadcasted_iota(jnp.int32, (N_HEADS_A, chunk), 0)
    incl = ii >= jj
    strict = ii > jj
    ixj = ii ^ jj
    eye2 = (ii == jj).astype(F32)
    nw = nw_ref[...]

    def col_pair(x, lane_a):
        a = jnp.sum(jnp.where(lane_c == lane_a, x, 0.0), axis=1, keepdims=True)
        b = jnp.sum(jnp.where(lane_c == lane_a + 1, x, 0.0), axis=1, keepdims=True)
        return jnp.where(first, a, b)

    def prep(p, c):
        rows = slice(c * chunk, (c + 1) * chunk)
        pc = slice(p * hw, (p + 1) * hw)
        h_a = head0 + 2 * p
        q2, k2, v2 = q_ref[rows, pc], k_ref[rows, pc], v_ref[rows, pc]
        beta2 = col_pair(gcol_ref[rows, 0:LANES], h_a)
        gc2 = col_pair(gcol_ref[rows, LANES:2 * LANES], N_HEADS_A + h_a)
        grc = grow_ref[N_HEADS_A:2 * N_HEADS_A, rows]
        gr_a = jnp.sum(jnp.where(row_h == h_a, grc, 0.0), axis=0, keepdims=True)
        gr_b = jnp.sum(jnp.where(row_h == h_a + 1, grc, 0.0), axis=0, keepdims=True)
        gr2 = jnp.concatenate([gr_a, gr_b], axis=1)
        gl2 = jnp.where(first_row, gr_a[:, chunk - 1:chunk], gr_b[:, chunk - 1:chunk])
        decay2 = jnp.where(incl, jnp.exp(gc2 - gr2), 0.0)
        eg2 = jnp.exp(gc2)
        k2b = k2.astype(BF16)
        kq = lax.dot_general(jnp.concatenate([k2b, q2.astype(BF16)], axis=0), _block_diag(k2b),
                             (((1,), (1,)), ((), ())), preferred_element_type=F32)
        kd2 = k2 * jnp.exp(gl2 - gc2)
        kdt = jnp.concatenate([kd2[:, :LANES].T, kd2[:, LANES:].T], axis=1)
        egl_ref[slot, p, c] = jnp.exp(gl2)
        return dict(
            p=p, c=c, lmat=jnp.where(strict, beta2 * kq[:chunk] * decay2, 0.0),
            ka=jnp.concatenate([kdt, kq[chunk:] * decay2], axis=0).astype(BF16),
            bv=_block_diag(beta2 * v2), bk=_block_diag((beta2 * eg2) * k2), qd=q2 * eg2)

    def local_stages(group):
        ts = None
        b = 1
        while b < chunk:
            sel = (ixj >= b) & (ixj < 2 * b)
            xs = [jnp.where(sel, l["lmat"], 0.0) for l in group]
            if ts is None:
                ts = [eye2 - x for x in xs]
            else:
                tbs = [t.astype(BF16) for t in ts]
                xts = [_pair_dot(x, tb) for x, tb in zip(xs, tbs)]
                ts = [t - _pair_dot(tb, xt) for t, tb, xt in zip(ts, tbs, xts)]
                yield
            b *= 2
        sols = []
        for l, t in zip(group, ts):
            tb = t.astype(BF16)
            sols.append((jnp.dot(tb, l["bv"], preferred_element_type=F32),
                         jnp.dot(tb, l["bk"], preferred_element_type=F32)))
        yield
        for l, (u0, wk) in zip(group, sols):
            pw = jnp.dot(l["ka"], _block_diag(wk), preferred_element_type=F32)
            nu = jnp.dot(l["ka"], _block_diag(u0), preferred_element_type=F32)
            pq_ref[slot, l["p"], l["c"]] = jnp.concatenate(
                [pw[:HEAD_DIM_A], l["qd"] - pw[HEAD_DIM_A:]], axis=0).astype(BF16)
            n_ref[slot, l["p"], l["c"]] = nu[:HEAD_DIM_A]
            oc_ref[slot, l["p"], l["c"]] = nu[HEAD_DIM_A:]
        yield

    def state_step(p, c, s2):
        res = jnp.dot(pq_ref[prev, p, c], _block_diag(s2), preferred_element_type=F32)
        o2 = res[HEAD_DIM_A:] + oc_ref[prev, p, c]
        rows = slice(c * chunk, (c + 1) * chunk)
        for hh in range(2):
            cs = slice(p * hw + hh * HEAD_DIM_A, p * hw + (hh + 1) * HEAD_DIM_A)
            o_ref[rows, cs] = (_rms(o2[:, cs.start - p * hw:cs.stop - p * hw], nw)
                               * _silu(z_ref[rows, cs])).astype(o_ref.dtype)
        return egl_ref[prev, p, c] * s2 + n_ref[prev, p, c] - res[:HEAD_DIM_A]

    fresh = ((step - 1) % n_t) == 0
    states = [jnp.where(fresh, 0.0, s_ref[p]) for p in range(n_pairs)]
    groups = [[] for _ in range(n_pairs)]
    for c in range(n_chunks):
        for p in range(n_pairs):
            states[p] = state_step(p, c, states[p])
        groups[0].append(prep(0, c))
    for p in range(n_pairs):
        s_ref[p] = states[p]
    for p in range(n_pairs):
        todo = list(range(n_chunks)) if p + 1 < n_pairs else []
        for _ in local_stages(groups[p]):
            if todo:
                groups[p + 1].append(prep(p + 1, todo.pop(0)))
        for c in todo:
            groups[p + 1].append(prep(p + 1, c))


def delta_mixer(proj, a_log, dt_bias, norm_w, *, batch, seq):
    heads, chunk = DELTA_HEADS, DELTA_CHUNK
    tt = min(DELTA_TILE, seq)
    hk = N_HEADS_A * HEAD_DIM_A
    hw = heads * HEAD_DIM_A
    n_hg = N_HEADS_A // heads
    n_t = seq // tt
    n_c = tt // chunk
    n_tiles = batch * n_hg * n_t
    m = batch * seq
    pad_row = jnp.zeros((1, LANES), F32)
    al = pad_row.at[0, N_HEADS_A:2 * N_HEADS_A].set(a_log.astype(F32))
    dtb = pad_row.at[0, N_HEADS_A:2 * N_HEADS_A].set(dt_bias.astype(F32))

    def block(off, lag):
        def index(s):
            tile = jnp.clip(s - lag, 0, n_tiles - 1)
            row, t = tile // n_t, tile % n_t
            return ((row // n_hg) * n_t + t, off * n_hg + row % n_hg)
        return index

    def gate_block(s):
        tile = jnp.minimum(s, n_tiles - 1)
        return (((tile // n_t) // n_hg) * n_t + tile % n_t, 4 * hk // LANES)

    const = lambda s: (0, 0)
    pw = 2 * HEAD_DIM_A
    n_pairs = heads // 2
    slots = lambda shape, dtype: pltpu.VMEM((2, n_pairs, n_c) + shape, dtype)
    return pl.pallas_call(
        functools.partial(_delta_kernel, chunk=chunk, n_t=n_t, n_hg=n_hg),
        grid=(n_tiles + 1,),
        in_specs=[pl.BlockSpec((tt, hw), block(0, 0)),
                  pl.BlockSpec((tt, hw), block(1, 0)),
                  pl.BlockSpec((tt, hw), block(2, 0)),
                  pl.BlockSpec((tt, hw), block(3, 1)),
                  pl.BlockSpec((tt, LANES), gate_block),
                  pl.BlockSpec((1, LANES), const),
                  pl.BlockSpec((1, LANES), const),
                  pl.BlockSpec((1, HEAD_DIM_A), const)],
        out_specs=pl.BlockSpec((tt, hw), block(0, 1)),
        out_shape=jax.ShapeDtypeStruct((m, hk), BF16),
        scratch_shapes=[pltpu.VMEM((tt, 2 * LANES), F32),
                        pltpu.VMEM((LANES, tt), F32),
                        slots((HEAD_DIM_A + chunk, pw), BF16),
                        slots((HEAD_DIM_A, pw), F32),
                        slots((chunk, pw), F32),
                        slots((1, pw), F32),
                        pltpu.VMEM((n_pairs, HEAD_DIM_A, pw), F32)],
        compiler_params=_cparams(("arbitrary",)),
        name="delta_mixer",
    )(proj, proj, proj, proj, proj, al, dtb, norm_w.reshape(1, HEAD_DIM_A))


def _swa_kernel(sink_ref, q_ref, kc_ref, vc_ref, kp_ref, vp_ref, o_ref, bias_ref, *, steps_per_seq, qblocks):
    w = WINDOW
    grp = N_HEADS_B // N_KV_B
    hd = HEAD_DIM_B
    step = pl.program_id(0)

    @pl.when(step == 0)
    def _():
        qi = lax.broadcasted_iota(jnp.int32, (w, 2 * w), 0)
        kj = lax.broadcasted_iota(jnp.int32, (w, 2 * w), 1)
        dist = qi + w - kj
        band = (dist >= 0) & (dist < w)
        distf = dist.astype(F32)
        for head in range(N_HEADS_B):
            slope = 2.0 ** (-8.0 * (head + 1) / N_HEADS_B)
            bias_ref[0, head] = jnp.where(band, -slope * distf, -jnp.inf)
            bias_ref[1, head] = jnp.where(band & (kj >= w), -slope * distf, -jnp.inf)

    first = ((step % steps_per_seq) == 0).astype(jnp.int32)
    scale = hd ** -0.5
    ones = jnp.ones((2 * w, 3 * hd), BF16)

    def scores(n, h):
        rows = slice(n * w, (n + 1) * w)
        cols = slice(h * hd, (h + 1) * hd)
        k_prev = kp_ref[:, cols] if n == 0 else kc_ref[(n - 1) * w:n * w, cols]
        kb = jnp.concatenate([k_prev, kc_ref[rows, cols]], axis=0).astype(BF16)
        qs = jnp.concatenate([q_ref[rows, (h * grp + g) * hd:(h * grp + g + 1) * hd] for g in range(grp)], axis=0)
        return lax.dot_general((qs * scale).astype(BF16), kb, (((1,), (1,)), ((), ())),
                               preferred_element_type=F32)

    items = [(n, h) for n in range(qblocks) for h in range(N_KV_B)]
    all_scores = [scores(n, h) for n, h in items]
    for (n, h), s in zip(items, all_scores):
        rows = slice(n * w, (n + 1) * w)
        cols = slice(h * hd, (h + 1) * hd)
        v_prev = vp_ref[:, cols] if n == 0 else vc_ref[(n - 1) * w:n * w, cols]
        vb = jnp.concatenate([v_prev, vc_ref[rows, cols]], axis=0).astype(BF16)
        vb1 = jnp.concatenate([vb, ones], axis=1)
        es = []
        sk = []
        for g in range(grp):
            head = h * grp + g
            sink = sink_ref[head]
            bias = bias_ref[first, head] if n == 0 else bias_ref[0, head]
            logits = s[g * w:(g + 1) * w] + bias
            mx = jnp.maximum(jnp.max(logits, axis=-1, keepdims=True), sink)
            es.append(jnp.exp(logits - mx).astype(BF16))
            sk.append(jnp.exp(sink - mx))
        pv = jnp.dot(jnp.concatenate(es, axis=0), vb1, preferred_element_type=F32)
        for g in range(grp):
            head = h * grp + g
            pg = pv[g * w:(g + 1) * w]
            den = pg[:, 2 * hd:3 * hd] + sk[g]
            o_ref[rows, head * hd:(head + 1) * hd] = (pg[:, :hd] / den).astype(o_ref.dtype)


def swa_mixer(proj, sinks, *, batch, seq):
    w = WINDOW
    qblocks = 4
    tq = qblocks * w
    qd = N_HEADS_B * HEAD_DIM_B
    kd = N_KV_B * HEAD_DIM_B
    m = batch * seq
    kcol, vcol = qd // kd, qd // kd + 1
    prev = lambda i: jnp.maximum(i * qblocks - 1, 0)
    return pl.pallas_call(
        functools.partial(_swa_kernel, steps_per_seq=seq // tq, qblocks=qblocks),
        grid=(m // tq,),
        in_specs=[pl.BlockSpec(memory_space=pltpu.SMEM),
                  pl.BlockSpec((tq, qd), lambda i: (i, 0)),
                  pl.BlockSpec((tq, kd), lambda i: (i, kcol)),
                  pl.BlockSpec((tq, kd), lambda i: (i, vcol)),
                  pl.BlockSpec((w, kd), lambda i: (prev(i), kcol)),
                  pl.BlockSpec((w, kd), lambda i: (prev(i), vcol))],
        out_specs=pl.BlockSpec((tq, qd), lambda i: (i, 0)),
        out_shape=jax.ShapeDtypeStruct((m, qd), BF16),
        scratch_shapes=[pltpu.VMEM((2, N_HEADS_B, w, 2 * w), F32)],
        compiler_params=_cparams(("arbitrary",)),
        name="swa_mixer",
    )(sinks.astype(F32), proj, proj, proj, proj, proj)


def kernel(x, p, norm_mix, norm_ffn, norm_ple, norm_final, a_w_in, a_conv, a_log, a_dt_bias, a_norm, a_w_out,
           b_w_in, b_sinks, b_w_out, f_w_up, f_conv, f_w_down, ple_w_proj, ple_w_gate):
    batch, seq, d = x.shape
    depth = p.shape[0]
    m = batch * seq
    hk = N_HEADS_A * HEAD_DIM_A
    h = x.reshape(m, d)
    for i in range(depth):
        j = i // 2
        if i % 2 == 0:
            w_in = a_w_in[j]
            w_in = jnp.pad(w_in, ((0, 0), (0, 4 * hk + LANES - w_in.shape[1]))).astype(BF16)
            conv_w = jnp.pad(a_conv[j], ((0, 0), (0, w_in.shape[1] - a_conv[j].shape[1])))
            proj = delta_in_proj(h, norm_mix[i], w_in, conv_w, tm=LAYER_TILE, seq=seq)
            mix = delta_mixer(proj, a_log[j], a_dt_bias[j], a_norm[j], batch=batch, seq=seq)
            w_out = a_w_out[j]
        else:
            proj = rms_matmul(h, norm_mix[i], b_w_in[j].astype(BF16), tm=512)
            mix = swa_mixer(proj, b_sinks[j], batch=batch, seq=seq)
            w_out = b_w_out[j]
        h = layer_tail(h, mix, p[i].reshape(m, -1), w_out.astype(BF16), norm_ffn[i], f_w_up[i].astype(BF16),
                       f_conv[i], f_w_down[i].astype(BF16), norm_ple[i], ple_w_gate[i].astype(BF16),
                       ple_w_proj[i].astype(BF16), norm_final, tm=LAYER_TILE, seq=seq, final=(i == depth - 1))
    return h.reshape(batch, seq, d)
```

```python
import functools
import math

import jax
import jax.numpy as jnp
from jax import lax
from jax.experimental import pallas as pl
from jax.experimental.pallas import tpu as pltpu

F32 = jnp.float32
BF16 = jnp.bfloat16

D_MODEL = 1024
N_HEADS_A = 8
HEAD_DIM_A = 128
CONV_A = 4
N_HEADS_B = 16
N_KV_B = 4
HEAD_DIM_B = 64
WINDOW = 128
D_FF = 2816
FFN_CONV = 3
PLE_DIM = 256
EPS = 1e-6

LANES = 128
BF16_ROWS = 16
VMEM_LIMIT = 56 * 1024 * 1024

CAST_STEPS = 16
LAYER_TILE = 512
DELTA_CHUNK = 128
DELTA_TILE = 1024
DELTA_HEADS = 4


def _cparams(sem):
    return pltpu.CompilerParams(dimension_semantics=sem, vmem_limit_bytes=VMEM_LIMIT)


def _rms(x, gain):
    ms = jnp.mean(x * x, axis=-1, keepdims=True)
    return x * lax.rsqrt(ms + EPS) * gain


def _sigmoid(x):
    return 0.5 + 0.5 * jnp.tanh(0.5 * x)


def _silu(x):
    h = 0.5 * x
    return h + h * jnp.tanh(h)


def _cast_kernel(*refs):
    n = len(refs) // 2
    for src, dst in zip(refs[:n], refs[n:]):
        w = src.shape[-1]
        dst[..., :w] = src[...].astype(dst.dtype)
        if dst.shape[-1] > w:
            dst[..., w:] = jnp.zeros(dst.shape[:-1] + (dst.shape[-1] - w,), dst.dtype)


def cast_weights(ws, widths):
    in_specs, out_specs, out_shape = [], [], []
    for w, width in zip(ws, widths):
        l, r, c = w.shape
        br = r // CAST_STEPS
        assert br * CAST_STEPS == r and br % BF16_ROWS == 0, w.shape
        in_specs.append(pl.BlockSpec((l, br, c), lambda i: (0, i, 0)))
        out_specs.append(pl.BlockSpec((l, br, width or c), lambda i: (0, i, 0)))
        out_shape.append(jax.ShapeDtypeStruct((l, r, width or c), BF16))
    return pl.pallas_call(
        _cast_kernel,
        grid=(CAST_STEPS,),
        in_specs=in_specs,
        out_specs=out_specs,
        out_shape=out_shape,
        compiler_params=_cparams(("parallel",)),
        name="cast_weights",
    )(*ws)


def _rms_matmul_kernel(x_ref, g_ref, w_ref, o_ref, *, n_chunk):
    xn = _rms(x_ref[...], g_ref[...]).astype(BF16)
    n = o_ref.shape[1]
    for c in range(0, n, n_chunk):
        w = min(n_chunk, n - c)
        o_ref[:, c:c + w] = jnp.dot(xn, w_ref[:, c:c + w], preferred_element_type=F32).astype(o_ref.dtype)


def rms_matmul(x, gain, w, *, tm, out_dtype=F32):
    m, d = x.shape
    n = w.shape[1]
    return pl.pallas_call(
        functools.partial(_rms_matmul_kernel, n_chunk=512),
        grid=(m // tm,),
        in_specs=[pl.BlockSpec((tm, d), lambda i: (i, 0)),
                  pl.BlockSpec((1, d), lambda i: (0, 0)),
                  pl.BlockSpec((d, n), lambda i: (0, 0))],
        out_specs=pl.BlockSpec((tm, n), lambda i: (i, 0)),
        out_shape=jax.ShapeDtypeStruct((m, n), out_dtype),
        compiler_params=_cparams(("parallel",)),
        name="rms_matmul",
    )(x, gain.reshape(1, d), w)


def _delta_in_kernel(x_ref, xh_ref, g_ref, w_ref, cw_ref, o_ref, *, tiles_per_seq, n_chunk):
    hb = xh_ref.shape[0]
    hk = N_HEADS_A * HEAD_DIM_A
    gain = g_ref[...]
    first = (pl.program_id(0) % tiles_per_seq) == 0
    xm = _rms(x_ref[...], gain).astype(BF16)
    xe = jnp.concatenate([jnp.where(first, 0.0, _rms(xh_ref[...], gain)).astype(BF16), xm], axis=0)
    for c in range(0, 3 * hk, n_chunk):
        u = jnp.dot(xe, w_ref[:, c:c + n_chunk], preferred_element_type=F32)
        w = cw_ref[:, c:c + n_chunk]
        y = w[CONV_A - 1:CONV_A, :] * u
        for i in range(CONV_A - 1):
            y = y + w[i:i + 1, :] * pltpu.roll(u, CONV_A - 1 - i, 0)
        y = _silu(y[hb:])
        if c < 2 * hk:
            for j in range(0, n_chunk, HEAD_DIM_A):
                yh = y[:, j:j + HEAD_DIM_A]
                yh = yh * lax.rsqrt(jnp.sum(yh * yh, axis=-1, keepdims=True) + EPS)
                if c < hk:
                    yh = yh * (HEAD_DIM_A ** -0.5)
                o_ref[:, c + j:c + j + HEAD_DIM_A] = yh
        else:
            o_ref[:, c:c + n_chunk] = y
    n = o_ref.shape[1]
    for c in range(3 * hk, n, n_chunk):
        wd = min(n_chunk, n - c)
        o_ref[:, c:c + wd] = jnp.dot(xm, w_ref[:, c:c + wd], preferred_element_type=F32)


def delta_in_proj(x, gain, w, conv_w, *, tm, seq):
    m, d = x.shape
    n = w.shape[1]
    hb = BF16_ROWS
    return pl.pallas_call(
        functools.partial(_delta_in_kernel, tiles_per_seq=seq // tm, n_chunk=512),
        grid=(m // tm,),
        in_specs=[pl.BlockSpec((tm, d), lambda i: (i, 0)),
                  pl.BlockSpec((hb, d), lambda i: (jnp.maximum(i * (tm // hb) - 1, 0), 0)),
                  pl.BlockSpec((1, d), lambda i: (0, 0)),
                  pl.BlockSpec((d, n), lambda i: (0, 0), pipeline_mode=pl.Buffered(1)),
                  pl.BlockSpec(conv_w.shape, lambda i: (0, 0))],
        out_specs=pl.BlockSpec((tm, n), lambda i: (i, 0)),
        out_shape=jax.ShapeDtypeStruct((m, n), F32),
        compiler_params=_cparams(("parallel",)),
        name="delta_in_proj",
    )(x, x, gain.reshape(1, d), w, conv_w)


def _layer_tail_kernel(h_ref, hh_ref, mix_ref, mixh_ref, p_ref, wo_ref, gf_ref, wup_ref, cw_ref, wd_ref,
                       gp_ref, wgate_ref, wp_ref, gl_ref, o_ref, *, tiles_per_seq, final):
    hb = hh_ref.shape[0]
    mix_ext = jnp.concatenate([mixh_ref[...], mix_ref[...]], axis=0)
    h_ext = jnp.concatenate([hh_ref[...], h_ref[...]], axis=0)
    h1 = h_ext + jnp.dot(mix_ext, wo_ref[...], preferred_element_type=F32)
    first = (pl.program_id(0) % tiles_per_seq) == 0
    gain = gf_ref[...]
    xn = jnp.concatenate([jnp.where(first, 0.0, _rms(h1[:hb], gain)).astype(BF16),
                          _rms(h1[hb:], gain).astype(BF16)], axis=0)

    def conv(col):
        u = jnp.dot(xn, wup_ref[:, col:col + D_FF], preferred_element_type=F32)
        w = cw_ref[:, col:col + D_FF]
        y = w[2:3, :] * u + w[1:2, :] * pltpu.roll(u, 1, 0) + w[0:1, :] * pltpu.roll(u, 2, 0)
        return y[hb:]

    act = (_silu(conv(0)) * conv(D_FF)).astype(BF16)
    h2 = h1[hb:] + jnp.dot(act, wd_ref[...], preferred_element_type=F32)

    gate = _sigmoid(jnp.dot(_rms(h2, gp_ref[...]).astype(BF16), wgate_ref[...], preferred_element_type=F32))
    emb = jnp.dot(p_ref[...].astype(BF16), wp_ref[...], preferred_element_type=F32)
    out = h2 + gate * emb
    if final:
        out = _rms(out, gl_ref[...])
    o_ref[...] = out


def layer_tail(h, mix, p, w_out, gain_ffn, w_up, conv_w, w_down, gain_ple, w_gate, w_proj, gain_last,
               *, tm, seq, final):
    m, d = h.shape
    pd = p.shape[1]
    hb = BF16_ROWS
    row = lambda i: (i, 0)
    halo = lambda i: (jnp.maximum(i * (tm // hb) - 1, 0), 0)
    const = lambda i: (0, 0)
    resident = dict(pipeline_mode=pl.Buffered(1))
    return pl.pallas_call(
        functools.partial(_layer_tail_kernel, tiles_per_seq=seq // tm, final=final),
        grid=(m // tm,),
        in_specs=[pl.BlockSpec((tm, d), row),
                  pl.BlockSpec((hb, d), halo),
                  pl.BlockSpec((tm, d), row),
                  pl.BlockSpec((hb, d), halo),
                  pl.BlockSpec((tm, pd), row),
                  pl.BlockSpec((d, d), const, **resident),
                  pl.BlockSpec((1, d), const),
                  pl.BlockSpec((d, 2 * D_FF), const, **resident),
                  pl.BlockSpec((FFN_CONV, 2 * D_FF), const),
                  pl.BlockSpec((D_FF, d), const, **resident),
                  pl.BlockSpec((1, d), const),
                  pl.BlockSpec((d, d), const, **resident),
                  pl.BlockSpec((pd, d), const, **resident),
                  pl.BlockSpec((1, d), const)],
        out_specs=pl.BlockSpec((tm, d), row),
        out_shape=jax.ShapeDtypeStruct((m, d), F32),
        compiler_params=_cparams(("parallel",)),
        name="layer_tail",
    )(h, h, mix, mix, p, w_out, gain_ffn.reshape(1, d), w_up, conv_w, w_down,
      gain_ple.reshape(1, d), w_gate, w_proj, gain_last.reshape(1, d))


def _block_diag(x):
    xb = x.astype(BF16)
    z = jnp.zeros((x.shape[0], LANES), BF16)
    top = jnp.concatenate([xb[:, :LANES], z], axis=1)
    bot = jnp.concatenate([z, xb[:, LANES:]], axis=1)
    return jnp.concatenate([top, bot], axis=0)


def _pair_dot(a, b):
    return jnp.dot(a.astype(BF16), _block_diag(b), preferred_element_type=F32)


def _delta_kernel(q_ref, k_ref, v_ref, z_ref, ba_ref, al_ref, dtb_ref, nw_ref, o_ref,
                  gcol_ref, grow_ref, pq_ref, n_ref, oc_ref, egl_ref, s_ref, *, chunk, n_t, n_hg):
    tt = q_ref.shape[0]
    hw = 2 * HEAD_DIM_A
    n_pairs = q_ref.shape[1] // hw
    step = pl.program_id(0)
    n_tiles = pl.num_programs(0) - 1
    n_chunks = tt // chunk
    slot = step % 2
    prev = 1 - slot
    tile = jnp.minimum(step, n_tiles - 1)
    head0 = ((tile // n_t) % n_hg) * (2 * n_pairs)

    @pl.when(step == 0)
    def _():
        pq_ref[1] = jnp.zeros(pq_ref.shape[1:], BF16)
        n_ref[1] = jnp.zeros(n_ref.shape[1:], F32)
        oc_ref[1] = jnp.zeros(oc_ref.shape[1:], F32)
        egl_ref[1] = jnp.zeros(egl_ref.shape[1:], F32)
        s_ref[...] = jnp.zeros(s_ref.shape, F32)

    ba = ba_ref[...]
    lane = lax.broadcasted_iota(jnp.int32, ba.shape, 1)
    sp = ba + dtb_ref[...]
    softplus = jnp.maximum(sp, 0.0) + jnp.log1p(jnp.exp(-jnp.abs(sp)))
    gb = jnp.where(lane < N_HEADS_A, _sigmoid(ba), -jnp.exp(al_ref[...]) * softplus)
    gcol_ref[:, 0:LANES] = gb
    grow_ref[...] = gb.T
    g8 = grow_ref[N_HEADS_A:2 * N_HEADS_A, :]
    tl = lax.broadcasted_iota(jnp.int32, g8.shape, 1) % chunk
    sh = 1
    while sh < chunk:
        g8 = g8 + jnp.where(tl >= sh, pltpu.roll(g8, sh, 1), 0.0)
        sh *= 2
    grow_ref[N_HEADS_A:2 * N_HEADS_A, :] = g8
    gcol_ref[:, LANES:2 * LANES] = grow_ref[...].T

    ii = lax.broadcasted_iota(jnp.int32, (chunk, hw), 0)
    lane2 = lax.broadcasted_iota(jnp.int32, (chunk, hw), 1)
    jj = lane2 % LANES
    first = lane2 < LANES
    first_row = lax.broadcasted_iota(jnp.int32, (1, hw), 1) < LANES
    lane_c = lax.broadcasted_iota(jnp.int32, (chunk, LANES), 1)
    row_h = lax.broadcasted_iota(jnp.int32, (N_HEADS_A, chunk), 0)
    incl = ii >= jj
    strict = ii > jj
    ixj = ii ^ jj
    eye2 = (ii == jj).astype(F32)
    nw = nw_ref[...]

    def col_pair(x, lane_a):
        a = jnp.sum(jnp.where(lane_c == lane_a, x, 0.0), axis=1, keepdims=True)
        b = jnp.sum(jnp.where(lane_c == lane_a + 1, x, 0.0), axis=1, keepdims=True)
        return jnp.where(first, a, b)

    def prep(p, c):
        rows = slice(c * chunk, (c + 1) * chunk)
        pc = slice(p * hw, (p + 1) * hw)
        h_a = head0 + 2 * p
        q2, k2, v2 = q_ref[rows, pc], k_ref[rows, pc], v_ref[rows, pc]
        beta2 = col_pair(gcol_ref[rows, 0:LANES], h_a)
        gc2 = col_pair(gcol_ref[rows, LANES:2 * LANES], N_HEADS_A + h_a)
        grc = grow_ref[N_HEADS_A:2 * N_HEADS_A, rows]
        gr_a = jnp.sum(jnp.where(row_h == h_a, grc, 0.0), axis=0, keepdims=True)
        gr_b = jnp.sum(jnp.where(row_h == h_a + 1, grc, 0.0), axis=0, keepdims=True)
        gr2 = jnp.concatenate([gr_a, gr_b], axis=1)
        gl2 = jnp.where(first_row, gr_a[:, chunk - 1:chunk], gr_b[:, chunk - 1:chunk])
        decay2 = jnp.where(incl, jnp.exp(gc2 - gr2), 0.0)
        eg2 = jnp.exp(gc2)
        k2b = k2.astype(BF16)
        kq = lax.dot_general(jnp.concatenate([k2b, q2.astype(BF16)], axis=0), _block_diag(k2b),
                             (((1,), (1,)), ((), ())), preferred_element_type=F32)
        kd2 = k2 * jnp.exp(gl2 - gc2)
        kdt = jnp.concatenate([kd2[:, :LANES].T, kd2[:, LANES:].T], axis=1)
        egl_ref[slot, p, c] = jnp.exp(gl2)
        return dict(
            p=p, c=c, lmat=jnp.where(strict, beta2 * kq[:chunk] * decay2, 0.0),
            ka=jnp.concatenate([kdt, kq[chunk:] * decay2], axis=0).astype(BF16),
            bv=_block_diag(beta2 * v2), bk=_block_diag((beta2 * eg2) * k2), qd=q2 * eg2)

    def local_stages(group):
        ts = None
        b = 1
        while b < chunk:
            sel = (ixj >= b) & (ixj < 2 * b)
            xs = [jnp.where(sel, l["lmat"], 0.0) for l in group]
            if ts is None:
                ts = [eye2 - x for x in xs]
            else:
                tbs = [t.astype(BF16) for t in ts]
                xts = [_pair_dot(x, tb) for x, tb in zip(xs, tbs)]
                ts = [t - _pair_dot(tb, xt) for t, tb, xt in zip(ts, tbs, xts)]
                yield
            b *= 2
        sols = []
        for l, t in zip(group, ts):
            tb = t.astype(BF16)
            sols.append((jnp.dot(tb, l["bv"], preferred_element_type=F32),
                         jnp.dot(tb, l["bk"], preferred_element_type=F32)))
        yield
        for l, (u0, wk) in zip(group, sols):
            pw = jnp.dot(l["ka"], _block_diag(wk), preferred_element_type=F32)
            nu = jnp.dot(l["ka"], _block_diag(u0), preferred_element_type=F32)
            pq_ref[slot, l["p"], l["c"]] = jnp.concatenate(
                [pw[:HEAD_DIM_A], l["qd"] - pw[HEAD_DIM_A:]], axis=0).astype(BF16)
            n_ref[slot, l["p"], l["c"]] = nu[:HEAD_DIM_A]
            oc_ref[slot, l["p"], l["c"]] = nu[HEAD_DIM_A:]
        yield

    def state_step(p, c, s2):
        res = jnp.dot(pq_ref[prev, p, c], _block_diag(s2), preferred_element_type=F32)
        o2 = res[HEAD_DIM_A:] + oc_ref[prev, p, c]
        rows = slice(c * chunk, (c + 1) * chunk)
        for hh in range(2):
            cs = slice(p * hw + hh * HEAD_DIM_A, p * hw + (hh + 1) * HEAD_DIM_A)
            o_ref[rows, cs] = (_rms(o2[:, cs.start - p * hw:cs.stop - p * hw], nw)
                               * _silu(z_ref[rows, cs])).astype(o_ref.dtype)
        return egl_ref[prev, p, c] * s2 + n_ref[prev, p, c] - res[:HEAD_DIM_A]

    fresh = ((step - 1) % n_t) == 0
    states = [jnp.where(fresh, 0.0, s_ref[p]) for p in range(n_pairs)]
    groups = [[] for _ in range(n_pairs)]
    for c in range(n_chunks):
        for p in range(n_pairs):
            states[p] = state_step(p, c, states[p])
        groups[0].append(prep(0, c))
    for p in range(n_pairs):
        s_ref[p] = states[p]
    for p in range(n_pairs):
        todo = list(range(n_chunks)) if p + 1 < n_pairs else []
        for _ in local_stages(groups[p]):
            if todo:
                groups[p + 1].append(prep(p + 1, todo.pop(0)))
        for c in todo:
            groups[p + 1].append(prep(p + 1, c))


def delta_mixer(proj, a_log, dt_bias, norm_w, *, batch, seq):
    heads, chunk = DELTA_HEADS, DELTA_CHUNK
    tt = min(DELTA_TILE, seq)
    hk = N_HEADS_A * HEAD_DIM_A
    hw = heads * HEAD_DIM_A
    n_hg = N_HEADS_A // heads
    n_t = seq // tt
    n_c = tt // chunk
    n_tiles = batch * n_hg * n_t
    m = batch * seq
    pad_row = jnp.zeros((1, LANES), F32)
    al = pad_row.at[0, N_HEADS_A:2 * N_HEADS_A].set(a_log.astype(F32))
    dtb = pad_row.at[0, N_HEADS_A:2 * N_HEADS_A].set(dt_bias.astype(F32))

    def block(off, lag):
        def index(s):
            tile = jnp.clip(s - lag, 0, n_tiles - 1)
            row, t = tile // n_t, tile % n_t
            return ((row // n_hg) * n_t + t, off * n_hg + row % n_hg)
        return index

    def gate_block(s):
        tile = jnp.minimum(s, n_tiles - 1)
        return (((tile // n_t) // n_hg) * n_t + tile % n_t, 4 * hk // LANES)

    const = lambda s: (0, 0)
    pw = 2 * HEAD_DIM_A
    n_pairs = heads // 2
    slots = lambda shape, dtype: pltpu.VMEM((2, n_pairs, n_c) + shape, dtype)
    return pl.pallas_call(
        functools.partial(_delta_kernel, chunk=chunk, n_t=n_t, n_hg=n_hg),
        grid=(n_tiles + 1,),
        in_specs=[pl.BlockSpec((tt, hw), block(0, 0)),
                  pl.BlockSpec((tt, hw), block(1, 0)),
                  pl.BlockSpec((tt, hw), block(2, 0)),
                  pl.BlockSpec((tt, hw), block(3, 1)),
                  pl.BlockSpec((tt, LANES), gate_block),
                  pl.BlockSpec((1, LANES), const),
                  pl.BlockSpec((1, LANES), const),
                  pl.BlockSpec((1, HEAD_DIM_A), const)],
        out_specs=pl.BlockSpec((tt, hw), block(0, 1)),
        out_shape=jax.ShapeDtypeStruct((m, hk), BF16),
        scratch_shapes=[pltpu.VMEM((tt, 2 * LANES), F32),
                        pltpu.VMEM((LANES, tt), F32),
                        slots((HEAD_DIM_A + chunk, pw), BF16),
                        slots((HEAD_DIM_A, pw), F32),
                        slots((chunk, pw), F32),
                        slots((1, pw), F32),
                        pltpu.VMEM((n_pairs, HEAD_DIM_A, pw), F32)],
        compiler_params=_cparams(("arbitrary",)),
        name="delta_mixer",
    )(proj, proj, proj, proj, proj, al, dtb, norm_w.reshape(1, HEAD_DIM_A))


def _swa_kernel(sink_ref, q_ref, kc_ref, vc_ref, kp_ref, vp_ref, o_ref, bias_ref, *, steps_per_seq, qblocks):
    w = WINDOW
    grp = N_HEADS_B // N_KV_B
    hd = HEAD_DIM_B
    step = pl.program_id(0)

    @pl.when(step == 0)
    def _():
        qi = lax.broadcasted_iota(jnp.int32, (w, 2 * w), 0)
        kj = lax.broadcasted_iota(jnp.int32, (w, 2 * w), 1)
        dist = qi + w - kj
        band = (dist >= 0) & (dist < w)
        distf = dist.astype(F32)
        for head in range(N_HEADS_B):
            slope = 2.0 ** (-8.0 * (head + 1) / N_HEADS_B)
            bias_ref[0, head] = jnp.where(band, -slope * distf, -jnp.inf)
            bias_ref[1, head] = jnp.where(band & (kj >= w), -slope * distf, -jnp.inf)

    first = ((step % steps_per_seq) == 0).astype(jnp.int32)
    scale = hd ** -0.5
    ones = jnp.ones((2 * w, 3 * hd), BF16)

    def scores(n, h):
        rows = slice(n * w, (n + 1) * w)
        cols = slice(h * hd, (h + 1) * hd)
        k_prev = kp_ref[:, cols] if n == 0 else kc_ref[(n - 1) * w:n * w, cols]
        kb = jnp.concatenate([k_prev, kc_ref[rows, cols]], axis=0).astype(BF16)
        qs = jnp.concatenate([q_ref[rows, (h * grp + g) * hd:(h * grp + g + 1) * hd] for g in range(grp)], axis=0)
        return lax.dot_general((qs * scale).astype(BF16), kb, (((1,), (1,)), ((), ())),
                               preferred_element_type=F32)

    items = [(n, h) for n in range(qblocks) for h in range(N_KV_B)]
    all_scores = [scores(n, h) for n, h in items]
    for (n, h), s in zip(items, all_scores):
        rows = slice(n * w, (n + 1) * w)
        cols = slice(h * hd, (h + 1) * hd)
        v_prev = vp_ref[:, cols] if n == 0 else vc_ref[(n - 1) * w:n * w, cols]
        vb = jnp.concatenate([v_prev, vc_ref[rows, cols]], axis=0).astype(BF16)
        vb1 = jnp.concatenate([vb, ones], axis=1)
        es = []
        sk = []
        for g in range(grp):
            head = h * grp + g
            sink = sink_ref[head]
            bias = bias_ref[first, head] if n == 0 else bias_ref[0, head]
            logits = s[g * w:(g + 1) * w] + bias
            mx = jnp.maximum(jnp.max(logits, axis=-1, keepdims=True), sink)
            es.append(jnp.exp(logits - mx).astype(BF16))
            sk.append(jnp.exp(sink - mx))
        pv = jnp.dot(jnp.concatenate(es, axis=0), vb1, preferred_element_type=F32)
        for g in range(grp):
            head = h * grp + g
            pg = pv[g * w:(g + 1) * w]
            den = pg[:, 2 * hd:3 * hd] + sk[g]
            o_ref[rows, head * hd:(head + 1) * hd] = (pg[:, :hd] / den).astype(o_ref.dtype)


def swa_mixer(proj, sinks, *, batch, seq):
    w = WINDOW
    qblocks = 4
    tq = qblocks * w
    qd = N_HEADS_B * HEAD_DIM_B
    kd = N_KV_B * HEAD_DIM_B
    m = batch * seq
    kcol, vcol = qd // kd, qd // kd + 1
    prev = lambda i: jnp.maximum(i * qblocks - 1, 0)
    return pl.pallas_call(
        functools.partial(_swa_kernel, steps_per_seq=seq // tq, qblocks=qblocks),
        grid=(m // tq,),
        in_specs=[pl.BlockSpec(memory_space=pltpu.SMEM),
                  pl.BlockSpec((tq, qd), lambda i: (i, 0)),
                  pl.BlockSpec((tq, kd), lambda i: (i, kcol)),
                  pl.BlockSpec((tq, kd), lambda i: (i, vcol)),
                  pl.BlockSpec((w, kd), lambda i: (prev(i), kcol)),
                  pl.BlockSpec((w, kd), lambda i: (prev(i), vcol))],
        out_specs=pl.BlockSpec((tq, qd), lambda i: (i, 0)),
        out_shape=jax.ShapeDtypeStruct((m, qd), BF16),
        scratch_shapes=[pltpu.VMEM((2, N_HEADS_B, w, 2 * w), F32)],
        compiler_params=_cparams(("arbitrary",)),
        name="swa_mixer",
    )(sinks.astype(F32), proj, proj, proj, proj, proj)


def kernel(x, p, norm_mix, norm_ffn, norm_ple, norm_final, a_w_in, a_conv, a_log, a_dt_bias, a_norm, a_w_out,
           b_w_in, b_sinks, b_w_out, f_w_up, f_conv, f_w_down, ple_w_proj, ple_w_gate):
    batch, seq, d = x.shape
    depth = p.shape[0]
    m = batch * seq
    hk = N_HEADS_A * HEAD_DIM_A
    h = x.reshape(m, d)
    a_in_b, a_out_b, b_in_b, b_out_b, up_b, down_b, proj_b, gate_b = cast_weights(
        (a_w_in, a_w_out, b_w_in, b_w_out, f_w_up, f_w_down, ple_w_proj, ple_w_gate),
        (4 * hk + LANES, None, None, None, None, None, None, None))
    for i in range(depth):
        j = i // 2
        if i % 2 == 0:
            conv_w = jnp.pad(a_conv[j], ((0, 0), (0, 4 * hk + LANES - a_conv[j].shape[1])))
            proj = delta_in_proj(h, norm_mix[i], a_in_b[j], conv_w, tm=LAYER_TILE, seq=seq)
            mix = delta_mixer(proj, a_log[j], a_dt_bias[j], a_norm[j], batch=batch, seq=seq)
            w_out = a_out_b[j]
        else:
            proj = rms_matmul(h, norm_mix[i], b_in_b[j], tm=LAYER_TILE)
            mix = swa_mixer(proj, b_sinks[j], batch=batch, seq=seq)
            w_out = b_out_b[j]
        h = layer_tail(h, mix, p[i].reshape(m, -1), w_out, norm_ffn[i], up_b[i], f_conv[i], down_b[i],
                       norm_ple[i], gate_b[i], proj_b[i], norm_final, tm=LAYER_TILE, seq=seq,
                       final=(i == depth - 1))
    return h.reshape(batch, seq, d)
```

```python
import functools
import math

import jax
import jax.numpy as jnp
from jax import lax
from jax.experimental import pallas as pl
from jax.experimental.pallas import tpu as pltpu

F32 = jnp.float32
BF16 = jnp.bfloat16

D_MODEL = 1024
N_HEADS_A = 8
HEAD_DIM_A = 128
CONV_A = 4
N_HEADS_B = 16
N_KV_B = 4
HEAD_DIM_B = 64
WINDOW = 128
D_FF = 2816
FFN_CONV = 3
PLE_DIM = 256
EPS = 1e-6
LOG2E = math.log2(math.e)

LANES = 128
BF16_ROWS = 16
VMEM_LIMIT = 56 * 1024 * 1024

CAST_STEPS = 16
LAYER_TILE = 512
DELTA_CHUNK = 128
DELTA_TILE = 1024
DELTA_HEADS = 4


def _cparams(sem):
    return pltpu.CompilerParams(dimension_semantics=sem, vmem_limit_bytes=VMEM_LIMIT)


def _rms(x, gain):
    ms = jnp.mean(x * x, axis=-1, keepdims=True)
    return x * lax.rsqrt(ms + EPS) * gain


def _sigmoid(x):
    return 0.5 + 0.5 * jnp.tanh(0.5 * x)


def _silu(x):
    h = 0.5 * x
    return h + h * jnp.tanh(h)


def _cast_kernel(*refs):
    n = len(refs) // 2
    for src, dst in zip(refs[:n], refs[n:]):
        w = src.shape[-1]
        dst[..., :w] = src[...].astype(dst.dtype)
        if dst.shape[-1] > w:
            dst[..., w:] = jnp.zeros(dst.shape[:-1] + (dst.shape[-1] - w,), dst.dtype)


def cast_weights(ws, widths):
    in_specs, out_specs, out_shape = [], [], []
    for w, width in zip(ws, widths):
        l, r, c = w.shape
        br = r // CAST_STEPS
        assert br * CAST_STEPS == r and br % BF16_ROWS == 0, w.shape
        in_specs.append(pl.BlockSpec((l, br, c), lambda i: (0, i, 0)))
        out_specs.append(pl.BlockSpec((l, br, width or c), lambda i: (0, i, 0)))
        out_shape.append(jax.ShapeDtypeStruct((l, r, width or c), BF16))
    return pl.pallas_call(
        _cast_kernel,
        grid=(CAST_STEPS,),
        in_specs=in_specs,
        out_specs=out_specs,
        out_shape=out_shape,
        compiler_params=_cparams(("parallel",)),
        name="cast_weights",
    )(*ws)


def _rms_matmul_kernel(x_ref, g_ref, w_ref, o_ref, *, n_chunk):
    xn = _rms(x_ref[...], g_ref[...]).astype(BF16)
    n = o_ref.shape[1]
    for c in range(0, n, n_chunk):
        w = min(n_chunk, n - c)
        o_ref[:, c:c + w] = jnp.dot(xn, w_ref[:, c:c + w], preferred_element_type=F32).astype(o_ref.dtype)


def _layer_spec(stack, layer, **kw):
    return pl.BlockSpec((None,) + stack.shape[1:], lambda i: (layer, 0, 0), **kw)


def rms_matmul(x, gain, w, layer, *, tm, out_dtype=F32):
    m, d = x.shape
    n = w.shape[-1]
    return pl.pallas_call(
        functools.partial(_rms_matmul_kernel, n_chunk=512),
        grid=(m // tm,),
        in_specs=[pl.BlockSpec((tm, d), lambda i: (i, 0)),
                  pl.BlockSpec((1, d), lambda i: (0, 0)),
                  _layer_spec(w, layer)],
        out_specs=pl.BlockSpec((tm, n), lambda i: (i, 0)),
        out_shape=jax.ShapeDtypeStruct((m, n), out_dtype),
        compiler_params=_cparams(("parallel",)),
        name="rms_matmul",
    )(x, gain.reshape(1, d), w)


def _delta_in_kernel(x_ref, xh_ref, g_ref, w_ref, cw_ref, o_ref, *, tiles_per_seq, n_chunk):
    hb = xh_ref.shape[0]
    hk = N_HEADS_A * HEAD_DIM_A
    gain = g_ref[...]
    first = (pl.program_id(0) % tiles_per_seq) == 0
    xm = _rms(x_ref[...], gain).astype(BF16)
    xe = jnp.concatenate([jnp.where(first, 0.0, _rms(xh_ref[...], gain)).astype(BF16), xm], axis=0)
    for c in range(0, 3 * hk, n_chunk):
        u = jnp.dot(xe, w_ref[:, c:c + n_chunk], preferred_element_type=F32)
        w = cw_ref[:, c:c + n_chunk]
        y = w[CONV_A - 1:CONV_A, :] * u
        for i in range(CONV_A - 1):
            y = y + w[i:i + 1, :] * pltpu.roll(u, CONV_A - 1 - i, 0)
        y = _silu(y[hb:])
        if c < 2 * hk:
            for j in range(0, n_chunk, HEAD_DIM_A):
                yh = y[:, j:j + HEAD_DIM_A]
                yh = yh * lax.rsqrt(jnp.sum(yh * yh, axis=-1, keepdims=True) + EPS)
                if c < hk:
                    yh = yh * (HEAD_DIM_A ** -0.5)
                o_ref[:, c + j:c + j + HEAD_DIM_A] = yh
        else:
            o_ref[:, c:c + n_chunk] = y
    n = o_ref.shape[1]
    for c in range(3 * hk, n, n_chunk):
        wd = min(n_chunk, n - c)
        o_ref[:, c:c + wd] = jnp.dot(xm, w_ref[:, c:c + wd], preferred_element_type=F32)


def delta_in_proj(x, gain, w, layer, conv_w, *, tm, seq):
    m, d = x.shape
    n = w.shape[-1]
    hb = BF16_ROWS
    return pl.pallas_call(
        functools.partial(_delta_in_kernel, tiles_per_seq=seq // tm, n_chunk=512),
        grid=(m // tm,),
        in_specs=[pl.BlockSpec((tm, d), lambda i: (i, 0)),
                  pl.BlockSpec((hb, d), lambda i: (jnp.maximum(i * (tm // hb) - 1, 0), 0)),
                  pl.BlockSpec((1, d), lambda i: (0, 0)),
                  _layer_spec(w, layer, pipeline_mode=pl.Buffered(1)),
                  pl.BlockSpec(conv_w.shape, lambda i: (0, 0))],
        out_specs=pl.BlockSpec((tm, n), lambda i: (i, 0)),
        out_shape=jax.ShapeDtypeStruct((m, n), F32),
        compiler_params=_cparams(("parallel",)),
        name="delta_in_proj",
    )(x, x, gain.reshape(1, d), w, conv_w)


def _layer_tail_kernel(h_ref, hh_ref, mix_ref, mixh_ref, p_ref, wo_ref, gf_ref, wup_ref, cw_ref, wd_ref,
                       gp_ref, wgate_ref, wp_ref, gl_ref, o_ref, *, tiles_per_seq, final):
    hb = hh_ref.shape[0]
    mix_ext = jnp.concatenate([mixh_ref[...], mix_ref[...]], axis=0)
    h_ext = jnp.concatenate([hh_ref[...], h_ref[...]], axis=0)
    h1 = h_ext + jnp.dot(mix_ext, wo_ref[...], preferred_element_type=F32)
    first = (pl.program_id(0) % tiles_per_seq) == 0
    gain = gf_ref[...]
    xn = jnp.concatenate([jnp.where(first, 0.0, _rms(h1[:hb], gain)).astype(BF16),
                          _rms(h1[hb:], gain).astype(BF16)], axis=0)

    def conv(col):
        u = jnp.dot(xn, wup_ref[:, col:col + D_FF], preferred_element_type=F32)
        w = cw_ref[:, col:col + D_FF]
        y = w[2:3, :] * u + w[1:2, :] * pltpu.roll(u, 1, 0) + w[0:1, :] * pltpu.roll(u, 2, 0)
        return y[hb:]

    act = (_silu(conv(0)) * conv(D_FF)).astype(BF16)
    h2 = h1[hb:] + jnp.dot(act, wd_ref[...], preferred_element_type=F32)

    gate = _sigmoid(jnp.dot(_rms(h2, gp_ref[...]).astype(BF16), wgate_ref[...], preferred_element_type=F32))
    emb = jnp.dot(p_ref[...].astype(BF16), wp_ref[...], preferred_element_type=F32)
    out = h2 + gate * emb
    if final:
        out = _rms(out, gl_ref[...])
    o_ref[...] = out


def layer_tail(h, mix, p, w_out, mixer_layer, gain_ffn, w_up, conv_w, w_down, gain_ple, w_gate, w_proj, gain_last,
               layer, *, tm, seq, final):
    m, d = h.shape
    pd = p.shape[-1]
    hb = BF16_ROWS
    row = lambda i: (i, 0)
    halo = lambda i: (jnp.maximum(i * (tm // hb) - 1, 0), 0)
    const = lambda i: (0, 0)
    resident = dict(pipeline_mode=pl.Buffered(1))
    return pl.pallas_call(
        functools.partial(_layer_tail_kernel, tiles_per_seq=seq // tm, final=final),
        grid=(m // tm,),
        in_specs=[pl.BlockSpec((tm, d), row),
                  pl.BlockSpec((hb, d), halo),
                  pl.BlockSpec((tm, d), row),
                  pl.BlockSpec((hb, d), halo),
                  pl.BlockSpec((None, tm, pd), lambda i: (layer, i, 0)),
                  _layer_spec(w_out, mixer_layer, **resident),
                  pl.BlockSpec((1, d), const),
                  _layer_spec(w_up, layer, **resident),
                  pl.BlockSpec((FFN_CONV, 2 * D_FF), const),
                  _layer_spec(w_down, layer, **resident),
                  pl.BlockSpec((1, d), const),
                  _layer_spec(w_gate, layer, **resident),
                  _layer_spec(w_proj, layer, **resident),
                  pl.BlockSpec((1, d), const)],
        out_specs=pl.BlockSpec((tm, d), row),
        out_shape=jax.ShapeDtypeStruct((m, d), F32),
        compiler_params=_cparams(("parallel",)),
        name="layer_tail",
    )(h, h, mix, mix, p, w_out, gain_ffn.reshape(1, d), w_up, conv_w, w_down,
      gain_ple.reshape(1, d), w_gate, w_proj, gain_last.reshape(1, d))


def _block_diag(x):
    xb = x.astype(BF16)
    z = jnp.zeros((x.shape[0], LANES), BF16)
    top = jnp.concatenate([xb[:, :LANES], z], axis=1)
    bot = jnp.concatenate([z, xb[:, LANES:]], axis=1)
    return jnp.concatenate([top, bot], axis=0)


def _pair_dot(a, b):
    return jnp.dot(a.astype(BF16), _block_diag(b), preferred_element_type=F32)


def _delta_kernel(q_ref, k_ref, v_ref, z_ref, ba_ref, al_ref, dtb_ref, nw_ref, o_ref,
                  gcol_ref, grow_ref, pq_ref, n_ref, oc_ref, egl_ref, s_ref, *, chunk, n_t, n_hg):
    tt = q_ref.shape[0]
    hw = 2 * HEAD_DIM_A
    n_pairs = q_ref.shape[1] // hw
    step = pl.program_id(0)
    n_tiles = pl.num_programs(0) - 1
    n_chunks = tt // chunk
    slot = step % 2
    prev = 1 - slot
    tile = jnp.minimum(step, n_tiles - 1)
    head0 = ((tile // n_t) % n_hg) * (2 * n_pairs)

    @pl.when(step == 0)
    def _():
        pq_ref[1] = jnp.zeros(pq_ref.shape[1:], BF16)
        n_ref[1] = jnp.zeros(n_ref.shape[1:], F32)
        oc_ref[1] = jnp.zeros(oc_ref.shape[1:], F32)
        egl_ref[1] = jnp.zeros(egl_ref.shape[1:], F32)
        s_ref[...] = jnp.zeros(s_ref.shape, F32)

    ba = ba_ref[...]
    lane = lax.broadcasted_iota(jnp.int32, ba.shape, 1)
    sp = ba + dtb_ref[...]
    softplus = jnp.maximum(sp, 0.0) + jnp.log1p(jnp.exp(-jnp.abs(sp)))
    gb = jnp.where(lane < N_HEADS_A, _sigmoid(ba), -jnp.exp(al_ref[...]) * softplus)
    gcol_ref[:, 0:LANES] = gb
    grow_ref[...] = gb.T
    g8 = grow_ref[N_HEADS_A:2 * N_HEADS_A, :]
    tl = lax.broadcasted_iota(jnp.int32, g8.shape, 1) % chunk
    sh = 1
    while sh < chunk:
        g8 = g8 + jnp.where(tl >= sh, pltpu.roll(g8, sh, 1), 0.0)
        sh *= 2
    grow_ref[N_HEADS_A:2 * N_HEADS_A, :] = g8
    gcol_ref[:, LANES:2 * LANES] = grow_ref[...].T

    ii = lax.broadcasted_iota(jnp.int32, (chunk, hw), 0)
    lane2 = lax.broadcasted_iota(jnp.int32, (chunk, hw), 1)
    jj = lane2 % LANES
    first = lane2 < LANES
    first_row = lax.broadcasted_iota(jnp.int32, (1, hw), 1) < LANES
    lane_c = lax.broadcasted_iota(jnp.int32, (chunk, LANES), 1)
    row_h = lax.broadcasted_iota(jnp.int32, (N_HEADS_A, chunk), 0)
    incl = ii >= jj
    strict = ii > jj
    ixj = ii ^ jj
    eye2 = (ii == jj).astype(F32)
    nw = nw_ref[...]

    def col_pair(x, lane_a):
        a = jnp.sum(jnp.where(lane_c == lane_a, x, 0.0), axis=1, keepdims=True)
        b = jnp.sum(jnp.where(lane_c == lane_a + 1, x, 0.0), axis=1, keepdims=True)
        return jnp.where(first, a, b)

    def prep(p, c):
        rows = slice(c * chunk, (c + 1) * chunk)
        pc = slice(p * hw, (p + 1) * hw)
        h_a = head0 + 2 * p
        q2, k2, v2 = q_ref[rows, pc], k_ref[rows, pc], v_ref[rows, pc]
        beta2 = col_pair(gcol_ref[rows, 0:LANES], h_a)
        gc2 = col_pair(gcol_ref[rows, LANES:2 * LANES], N_HEADS_A + h_a)
        grc = grow_ref[N_HEADS_A:2 * N_HEADS_A, rows]
        gr_a = jnp.sum(jnp.where(row_h == h_a, grc, 0.0), axis=0, keepdims=True)
        gr_b = jnp.sum(jnp.where(row_h == h_a + 1, grc, 0.0), axis=0, keepdims=True)
        gr2 = jnp.concatenate([gr_a, gr_b], axis=1)
        gl2 = jnp.where(first_row, gr_a[:, chunk - 1:chunk], gr_b[:, chunk - 1:chunk])
        decay2 = jnp.where(incl, jnp.exp(gc2 - gr2), 0.0)
        eg2 = jnp.exp(gc2)
        k2b = k2.astype(BF16)
        kq = lax.dot_general(jnp.concatenate([k2b, q2.astype(BF16)], axis=0), _block_diag(k2b),
                             (((1,), (1,)), ((), ())), preferred_element_type=F32)
        kd2 = k2 * jnp.exp(gl2 - gc2)
        kdt = jnp.concatenate([kd2[:, :LANES].T, kd2[:, LANES:].T], axis=1)
        egl_ref[slot, p, c] = jnp.exp(gl2)
        return dict(
            p=p, c=c, lmat=jnp.where(strict, beta2 * kq[:chunk] * decay2, 0.0),
            ka=jnp.concatenate([kdt, kq[chunk:] * decay2], axis=0).astype(BF16),
            bv=_block_diag(beta2 * v2), bk=_block_diag((beta2 * eg2) * k2), qd=q2 * eg2)

    def local_stages(group):
        ts = None
        b = 1
        while b < chunk:
            sel = (ixj >= b) & (ixj < 2 * b)
            xs = [jnp.where(sel, l["lmat"], 0.0) for l in group]
            if ts is None:
                ts = [eye2 - x for x in xs]
            else:
                tbs = [t.astype(BF16) for t in ts]
                xts = [_pair_dot(x, tb) for x, tb in zip(xs, tbs)]
                ts = [t - _pair_dot(tb, xt) for t, tb, xt in zip(ts, tbs, xts)]
                yield
            b *= 2
        sols = []
        for l, t in zip(group, ts):
            tb = t.astype(BF16)
            sols.append((jnp.dot(tb, l["bv"], preferred_element_type=F32),
                         jnp.dot(tb, l["bk"], preferred_element_type=F32)))
        yield
        for l, (u0, wk) in zip(group, sols):
            pw = jnp.dot(l["ka"], _block_diag(wk), preferred_element_type=F32)
            nu = jnp.dot(l["ka"], _block_diag(u0), preferred_element_type=F32)
            pq_ref[slot, l["p"], l["c"]] = jnp.concatenate(
                [pw[:HEAD_DIM_A], l["qd"] - pw[HEAD_DIM_A:]], axis=0).astype(BF16)
            n_ref[slot, l["p"], l["c"]] = nu[:HEAD_DIM_A]
            oc_ref[slot, l["p"], l["c"]] = nu[HEAD_DIM_A:]
        yield

    def state_step(p, c, s2):
        res = jnp.dot(pq_ref[prev, p, c], _block_diag(s2), preferred_element_type=F32)
        o2 = res[HEAD_DIM_A:] + oc_ref[prev, p, c]
        rows = slice(c * chunk, (c + 1) * chunk)
        for hh in range(2):
            cs = slice(p * hw + hh * HEAD_DIM_A, p * hw + (hh + 1) * HEAD_DIM_A)
            o_ref[rows, cs] = (_rms(o2[:, cs.start - p * hw:cs.stop - p * hw], nw)
                               * _silu(z_ref[rows, cs])).astype(o_ref.dtype)
        return egl_ref[prev, p, c] * s2 + n_ref[prev, p, c] - res[:HEAD_DIM_A]

    fresh = ((step - 1) % n_t) == 0
    states = [jnp.where(fresh, 0.0, s_ref[p]) for p in range(n_pairs)]
    groups = [[] for _ in range(n_pairs)]
    for c in range(n_chunks):
        for p in range(n_pairs):
            states[p] = state_step(p, c, states[p])
        groups[0].append(prep(0, c))
    for p in range(n_pairs):
        s_ref[p] = states[p]
    for p in range(n_pairs):
        todo = list(range(n_chunks)) if p + 1 < n_pairs else []
        for _ in local_stages(groups[p]):
            if todo:
                groups[p + 1].append(prep(p + 1, todo.pop(0)))
        for c in todo:
            groups[p + 1].append(prep(p + 1, c))


def delta_mixer(proj, a_log, dt_bias, norm_w, *, batch, seq):
    heads, chunk = DELTA_HEADS, DELTA_CHUNK
    tt = min(DELTA_TILE, seq)
    hk = N_HEADS_A * HEAD_DIM_A
    hw = heads * HEAD_DIM_A
    n_hg = N_HEADS_A // heads
    n_t = seq // tt
    n_c = tt // chunk
    n_tiles = batch * n_hg * n_t
    m = batch * seq
    pad_row = jnp.zeros((1, LANES), F32)
    al = pad_row.at[0, N_HEADS_A:2 * N_HEADS_A].set(a_log.astype(F32))
    dtb = pad_row.at[0, N_HEADS_A:2 * N_HEADS_A].set(dt_bias.astype(F32))

    def block(off, lag):
        def index(s):
            tile = jnp.clip(s - lag, 0, n_tiles - 1)
            row, t = tile // n_t, tile % n_t
            return ((row // n_hg) * n_t + t, off * n_hg + row % n_hg)
        return index

    def gate_block(s):
        tile = jnp.minimum(s, n_tiles - 1)
        return (((tile // n_t) // n_hg) * n_t + tile % n_t, 4 * hk // LANES)

    const = lambda s: (0, 0)
    pw = 2 * HEAD_DIM_A
    n_pairs = heads // 2
    slots = lambda shape, dtype: pltpu.VMEM((2, n_pairs, n_c) + shape, dtype)
    return pl.pallas_call(
        functools.partial(_delta_kernel, chunk=chunk, n_t=n_t, n_hg=n_hg),
        grid=(n_tiles + 1,),
        in_specs=[pl.BlockSpec((tt, hw), block(0, 0)),
                  pl.BlockSpec((tt, hw), block(1, 0)),
                  pl.BlockSpec((tt, hw), block(2, 0)),
                  pl.BlockSpec((tt, hw), block(3, 1)),
                  pl.BlockSpec((tt, LANES), gate_block),
                  pl.BlockSpec((1, LANES), const),
                  pl.BlockSpec((1, LANES), const),
                  pl.BlockSpec((1, HEAD_DIM_A), const)],
        out_specs=pl.BlockSpec((tt, hw), block(0, 1)),
        out_shape=jax.ShapeDtypeStruct((m, hk), BF16),
        scratch_shapes=[pltpu.VMEM((tt, 2 * LANES), F32),
                        pltpu.VMEM((LANES, tt), F32),
                        slots((HEAD_DIM_A + chunk, pw), BF16),
                        slots((HEAD_DIM_A, pw), F32),
                        slots((chunk, pw), F32),
                        slots((1, pw), F32),
                        pltpu.VMEM((n_pairs, HEAD_DIM_A, pw), F32)],
        compiler_params=_cparams(("arbitrary",)),
        name="delta_mixer",
    )(proj, proj, proj, proj, proj, al, dtb, norm_w.reshape(1, HEAD_DIM_A))


def _swa_kernel(sink_ref, q_ref, kc_ref, vc_ref, kp_ref, vp_ref, o_ref, bias_ref, *, steps_per_seq, qblocks):
    w = WINDOW
    grp = N_HEADS_B // N_KV_B
    hd = HEAD_DIM_B
    step = pl.program_id(0)

    @pl.when(step == 0)
    def _():
        qi = lax.broadcasted_iota(jnp.int32, (w, 2 * w), 0)
        kj = lax.broadcasted_iota(jnp.int32, (w, 2 * w), 1)
        dist = qi + w - kj
        band = (dist >= 0) & (dist < w)
        distf = dist.astype(F32)
        for head in range(N_HEADS_B):
            slope = 2.0 ** (-8.0 * (head + 1) / N_HEADS_B)
            bias_ref[0, head] = jnp.where(band, -(slope * LOG2E) * distf, -jnp.inf)
            bias_ref[1, head] = jnp.where(band & (kj >= w), -(slope * LOG2E) * distf, -jnp.inf)

    first = ((step % steps_per_seq) == 0).astype(jnp.int32)
    scale = hd ** -0.5
    ones = jnp.ones((2 * w, 3 * hd), BF16)

    def scores(n, h):
        rows = slice(n * w, (n + 1) * w)
        cols = slice(h * hd, (h + 1) * hd)
        k_prev = kp_ref[:, cols] if n == 0 else kc_ref[(n - 1) * w:n * w, cols]
        kb = jnp.concatenate([k_prev, kc_ref[rows, cols]], axis=0).astype(BF16)
        qs = jnp.concatenate([q_ref[rows, (h * grp + g) * hd:(h * grp + g + 1) * hd] for g in range(grp)], axis=0)
        return lax.dot_general((qs.astype(F32) * (scale * LOG2E)).astype(BF16), kb, (((1,), (1,)), ((), ())),
                               preferred_element_type=F32)

    items = [(n, h) for n in range(qblocks) for h in range(N_KV_B)]
    all_scores = [scores(n, h) for n, h in items]
    for (n, h), s in zip(items, all_scores):
        rows = slice(n * w, (n + 1) * w)
        cols = slice(h * hd, (h + 1) * hd)
        v_prev = vp_ref[:, cols] if n == 0 else vc_ref[(n - 1) * w:n * w, cols]
        vb = jnp.concatenate([v_prev, vc_ref[rows, cols]], axis=0).astype(BF16)
        vb1 = jnp.concatenate([vb, ones], axis=1)
        es = []
        sk = []
        for g in range(grp):
            head = h * grp + g
            sink = sink_ref[head] * LOG2E
            bias = bias_ref[first, head] if n == 0 else bias_ref[0, head]
            logits = s[g * w:(g + 1) * w] + bias
            mx = jnp.maximum(jnp.max(logits, axis=-1, keepdims=True), sink)
            es.append(jnp.exp2(logits - mx).astype(BF16))
            sk.append(jnp.exp2(sink - mx))
        pv = jnp.dot(jnp.concatenate(es, axis=0), vb1, preferred_element_type=F32)
        for g in range(grp):
            head = h * grp + g
            pg = pv[g * w:(g + 1) * w]
            den = pg[:, 2 * hd:3 * hd] + sk[g]
            o_ref[rows, head * hd:(head + 1) * hd] = (pg[:, :hd] / den).astype(o_ref.dtype)


def swa_mixer(proj, sinks, *, batch, seq):
    w = WINDOW
    qblocks = 4
    tq = qblocks * w
    qd = N_HEADS_B * HEAD_DIM_B
    kd = N_KV_B * HEAD_DIM_B
    m = batch * seq
    kcol, vcol = qd // kd, qd // kd + 1
    prev = lambda i: jnp.maximum(i * qblocks - 1, 0)
    return pl.pallas_call(
        functools.partial(_swa_kernel, steps_per_seq=seq // tq, qblocks=qblocks),
        grid=(m // tq,),
        in_specs=[pl.BlockSpec(memory_space=pltpu.SMEM),
                  pl.BlockSpec((tq, qd), lambda i: (i, 0)),
                  pl.BlockSpec((tq, kd), lambda i: (i, kcol)),
                  pl.BlockSpec((tq, kd), lambda i: (i, vcol)),
                  pl.BlockSpec((w, kd), lambda i: (prev(i), kcol)),
                  pl.BlockSpec((w, kd), lambda i: (prev(i), vcol))],
        out_specs=pl.BlockSpec((tq, qd), lambda i: (i, 0)),
        out_shape=jax.ShapeDtypeStruct((m, qd), BF16),
        scratch_shapes=[pltpu.VMEM((2, N_HEADS_B, w, 2 * w), F32)],
        compiler_params=_cparams(("arbitrary",)),
        name="swa_mixer",
    )(sinks.astype(F32), proj, proj, proj, proj, proj)


def kernel(x, p, norm_mix, norm_ffn, norm_ple, norm_final, a_w_in, a_conv, a_log, a_dt_bias, a_norm, a_w_out,
           b_w_in, b_sinks, b_w_out, f_w_up, f_conv, f_w_down, ple_w_proj, ple_w_gate):
    batch, seq, d = x.shape
    depth = p.shape[0]
    m = batch * seq
    hk = N_HEADS_A * HEAD_DIM_A
    h = x.reshape(m, d)
    a_in_b, a_out_b, b_in_b, b_out_b, up_b, down_b, proj_b, gate_b = cast_weights(
        (a_w_in, a_w_out, b_w_in, b_w_out, f_w_up, f_w_down, ple_w_proj, ple_w_gate),
        (4 * hk + LANES, None, None, None, None, None, None, None))
    p2 = p.reshape(depth, m, p.shape[-1])
    for i in range(depth):
        j = i // 2
        if i % 2 == 0:
            conv_w = jnp.pad(a_conv[j], ((0, 0), (0, 4 * hk + LANES - a_conv[j].shape[1])))
            proj = delta_in_proj(h, norm_mix[i], a_in_b, j, conv_w, tm=LAYER_TILE, seq=seq)
            mix = delta_mixer(proj, a_log[j], a_dt_bias[j], a_norm[j], batch=batch, seq=seq)
            w_out = a_out_b
        else:
            proj = rms_matmul(h, norm_mix[i], b_in_b, j, tm=LAYER_TILE, out_dtype=BF16)
            mix = swa_mixer(proj, b_sinks[j], batch=batch, seq=seq)
            w_out = b_out_b
        h = layer_tail(h, mix, p2, w_out, j, norm_ffn[i], up_b, f_conv[i], down_b, norm_ple[i], gate_b, proj_b,
                       norm_final, i, tm=LAYER_TILE, seq=seq, final=(i == depth - 1))
    return h.reshape(batch, seq, d)
```
